```python
import jax, jax.numpy as jnp
from jax import lax
import numpy as np

D_MODEL = 4096
BATCH = 2
SEQ = 4096
DEPTH = 1
DEC_BATCH = 128
DEC_SEQ = 1
PAST_LEN = 2048
PAGE_SIZE = 128

HEAD_DIM = 128
HEADS_PER_GROUP = 8
DILATED_GROUPS = ((128, 1), (512, 4), (2048, 16))
N_GROUPS = 3
ATTN_WIDTH = N_GROUPS * HEADS_PER_GROUP * HEAD_DIM
ATTN_OUT_WIDTH = HEADS_PER_GROUP * HEAD_DIM
CONV_WIDTH = D_MODEL
CONV_K = 3
IN_WIDTHS = (ATTN_WIDTH, ATTN_WIDTH, ATTN_WIDTH, CONV_WIDTH, CONV_WIDTH, CONV_WIDTH, D_MODEL, D_MODEL)
IN_WIDTH = 3 * ATTN_WIDTH + 3 * CONV_WIDTH + 2 * D_MODEL
PEER_HEADS = 8
PEER_N_KEYS = 128
PEER_N_EXPERTS = PEER_N_KEYS * PEER_N_KEYS
PEER_KEY_DIM = 256
PEER_HALF = PEER_KEY_DIM // 2
PEER_TOPK = 16
Q_BLOCK = 128
PEER_TOKEN_BLOCK = 64
EPS = 1e-6

kernel_name = 'dilated_attn_shortconv_peer_decode_step'


def rms_norm(x, gain):
    xf = x.astype(jnp.float32)
    y = xf * lax.rsqrt(jnp.mean(xf * xf, axis=-1, keepdims=True) + EPS)
    return (y * gain.astype(jnp.float32)).astype(x.dtype)


def dilated_attn_block(q, keys, vals, q_rows, dilation, n_taps):
    rows = q_rows[:, None] - dilation * jnp.arange(n_taps)[None, :]
    valid = rows >= 0
    rows = jnp.maximum(rows, 0)
    k_sel = jnp.take(keys, rows, axis=1)
    v_sel = jnp.take(vals, rows, axis=1)
    s = jnp.einsum('bqhd,bqjhd->bqhj', q, k_sel, preferred_element_type=jnp.float32) * (HEAD_DIM ** -0.5)
    s = jnp.where(valid[None, :, None, :], s, -jnp.inf)
    lse = jax.nn.logsumexp(s, axis=-1)
    p = jnp.exp(s - lse[..., None]).astype(vals.dtype)
    out = jnp.einsum('bqhj,bqjhd->bqhd', p, v_sel)
    return out, lse


def dilated_attn_prompt(q, k, v, dilation, n_taps):
    b, t, h, hd = q.shape
    qb = min(Q_BLOCK, t)
    nb = t // qb

    def blk(i):
        s0 = i * qb
        q_blk = lax.dynamic_slice_in_dim(q, s0, qb, axis=1)
        return dilated_attn_block(q_blk, k, v, s0 + jnp.arange(qb), dilation, n_taps)

    out, lse = lax.map(blk, jnp.arange(nb))
    out = jnp.moveaxis(out, 0, 1).reshape(b, t, h, hd)
    lse = jnp.moveaxis(lse, 0, 1).reshape(b, t, h)
    return out, lse


def combine_groups(outs, lses):
    outs = jnp.stack(outs, axis=0)
    wts = jax.nn.softmax(jnp.stack(lses, axis=0), axis=0).astype(outs.dtype)
    return jnp.sum(wts[..., None] * outs, axis=0)


def short_conv(u, prev, conv_w):
    t = u.shape[1]
    ext = jnp.concatenate([prev.astype(u.dtype), u], axis=1)
    y = conv_w[0] * ext[:, 0:t]
    for i in range(1, CONV_K):
        y = y + conv_w[i] * ext[:, i:i + t]
    return y, ext[:, t:]


def peer_ffn(h, w_peer_q, peer_subkeys, peer_u, peer_v):
    n, d = h.shape
    q = (h @ w_peer_q).reshape(n, PEER_HEADS, 2, PEER_HALF)
    s = jnp.einsum('nhpk,hpek->nhpe', q, peer_subkeys, preferred_element_type=jnp.float32)
    s1, i1 = lax.top_k(s[:, :, 0], PEER_TOPK)
    s2, i2 = lax.top_k(s[:, :, 1], PEER_TOPK)
    cand = (s1[..., :, None] + s2[..., None, :]).reshape(n, PEER_HEADS, PEER_TOPK * PEER_TOPK)
    cand_id = (i1[..., :, None] * PEER_N_KEYS + i2[..., None, :]).reshape(n, PEER_HEADS, PEER_TOPK * PEER_TOPK)
    top_s, top_pos = lax.top_k(cand, PEER_TOPK)
    expert_id = jnp.take_along_axis(cand_id, top_pos, axis=-1)
    gate = jax.nn.softmax(top_s, axis=-1).astype(h.dtype)
    tb = PEER_TOKEN_BLOCK
    nb = -(-n // tb)
    pad = nb * tb - n
    hb = jnp.pad(h, ((0, pad), (0, 0))).reshape(nb, tb, d)
    eb = jnp.pad(expert_id, ((0, pad), (0, 0), (0, 0))).reshape(nb, tb, PEER_HEADS, PEER_TOPK)
    gb = jnp.pad(gate, ((0, pad), (0, 0), (0, 0))).reshape(nb, tb, PEER_HEADS, PEER_TOPK)

    def blk(args):
        hh, ee, gg = args
        u_sel = jnp.take(peer_u, ee, axis=0)
        act = jax.nn.gelu(jnp.einsum('td,thkd->thk', hh, u_sel), approximate=False)
        v_sel = jnp.take(peer_v, ee, axis=0)
        return jnp.einsum('thk,thkd->td', gg * act, v_sel)

    out = lax.map(blk, (hb, eb, gb)).reshape(nb * tb, d)
    return out[:n]


def hybrid_layer(x, kv_caches, conv_prev, norm1_gain, w_in, q_norm_gain, k_norm_gain, conv_w,
                 w_attn_out, w_conv_out, w_out, norm2_gain, w_peer_q, peer_subkeys, peer_u, peer_v):
    b, t, _ = x.shape
    xn = rms_norm(x, norm1_gain)
    proj = jnp.einsum('btd,de->bte', xn, w_in)
    splits = np.cumsum(IN_WIDTHS)[:-1].tolist()
    q, k, v, conv_in, gate_b, gate_c, gate_attn, gate_conv = jnp.split(proj, splits, axis=-1)
    shp = (b, t, N_GROUPS, HEADS_PER_GROUP, HEAD_DIM)
    q = rms_norm(q.reshape(shp), q_norm_gain)
    k = rms_norm(k.reshape(shp), k_norm_gain)
    v = v.reshape(shp)
    outs, lses, new_kv = [], [], []
    for g, (window, dil) in enumerate(DILATED_GROUPS):
        n_taps = window // dil + 1
        kg, vg = k[:, :, g], v[:, :, g]
        if kv_caches is None:
            o, l = dilated_attn_prompt(q[:, :, g], kg, vg, dil, n_taps)
            keep = min(window, t)
            new_kv.append(jnp.stack([kg[:, t - keep:], vg[:, t - keep:]], axis=2))
        else:
            cache = kv_caches[g]
            past = cache.shape[1]
            keys = jnp.concatenate([cache[:, :, 0].astype(kg.dtype), kg], axis=1)
            vals = jnp.concatenate([cache[:, :, 1].astype(vg.dtype), vg], axis=1)
            o, l = dilated_attn_block(q[:, :, g], keys, vals, past + jnp.arange(t), dil, n_taps)
            new_kv.append(jnp.stack([kg, vg], axis=2))
        outs.append(o)
        lses.append(l)
    attn = combine_groups(outs, lses).reshape(b, t, ATTN_OUT_WIDTH)
    if conv_prev is None:
        conv_prev = jnp.zeros((b, CONV_K - 1, CONV_WIDTH), x.dtype)
    conv_y, new_conv = short_conv(gate_c * conv_in, conv_prev, conv_w)
    conv_branch = gate_b * conv_y
    merged = (jax.nn.sigmoid(gate_attn) * (attn @ w_attn_out)
              + jax.nn.sigmoid(gate_conv) * (conv_branch @ w_conv_out))
    x = x + merged @ w_out
    hn = rms_norm(x, norm2_gain)
    x = x + peer_ffn(hn.reshape(b * t, D_MODEL), w_peer_q, peer_subkeys, peer_u, peer_v).reshape(b, t, D_MODEL)
    return x, new_kv, new_conv


def setup_inputs(seed: int = 0) -> dict:
    key = jax.random.key(seed)
    ks = jax.random.split(key, 24)
    f32 = jnp.float32

    def nrm(k, shape, scale):
        return jax.random.normal(k, shape, f32) * scale

    buf = [min(w, PAST_LEN) for (w, _) in DILATED_GROUPS]
    kv_shape = lambda L: (DEPTH, DEC_BATCH, L, 2, HEADS_PER_GROUP, HEAD_DIM)
    return {
        'x_prompt': nrm(ks[0], (BATCH, SEQ, D_MODEL), 1.0),
        'x_sample': nrm(ks[1], (DEC_BATCH, DEC_SEQ, D_MODEL), 1.0),
        'cache_kv_w128': nrm(ks[2], kv_shape(buf[0]), 1.0),
        'cache_kv_w512': nrm(ks[3], kv_shape(buf[1]), 1.0),
        'cache_kv_w2048': nrm(ks[4], kv_shape(buf[2]), 1.0),
        'state_conv': nrm(ks[5], (DEPTH, DEC_BATCH, CONV_K - 1, CONV_WIDTH), 1.0),
        'norm1_gain': 1.0 + nrm(ks[6], (DEPTH, D_MODEL), 0.01),
        'w_in': nrm(ks[7], (DEPTH, D_MODEL, IN_WIDTH), D_MODEL ** -0.5),
        'q_norm_gain': 1.0 + nrm(ks[8], (DEPTH, HEAD_DIM), 0.01),
        'k_norm_gain': 1.0 + nrm(ks[9], (DEPTH, HEAD_DIM), 0.01),
        'conv_w': nrm(ks[10], (DEPTH, CONV_K, CONV_WIDTH), CONV_K ** -0.5),
        'w_attn_out': nrm(ks[11], (DEPTH, ATTN_OUT_WIDTH, D_MODEL), ATTN_OUT_WIDTH ** -0.5),
        'w_conv_out': nrm(ks[12], (DEPTH, CONV_WIDTH, D_MODEL), CONV_WIDTH ** -0.5),
        'w_out': nrm(ks[13], (DEPTH, D_MODEL, D_MODEL), D_MODEL ** -0.5),
        'norm2_gain': 1.0 + nrm(ks[14], (DEPTH, D_MODEL), 0.01),
        'w_peer_q': nrm(ks[15], (DEPTH, D_MODEL, PEER_HEADS * PEER_KEY_DIM), D_MODEL ** -0.5),
        'peer_subkeys': nrm(ks[16], (DEPTH, PEER_HEADS, 2, PEER_N_KEYS, PEER_HALF), PEER_HALF ** -0.5),
        'peer_u': nrm(ks[17], (DEPTH, PEER_N_EXPERTS, D_MODEL), D_MODEL ** -0.5),
        'peer_v': nrm(ks[18], (DEPTH, PEER_N_EXPERTS, D_MODEL), PEER_HEADS ** -0.5),
    }


def reference(x_prompt, x_sample, cache_kv_w128, cache_kv_w512, cache_kv_w2048, state_conv,
              norm1_gain, w_in, q_norm_gain, k_norm_gain, conv_w, w_attn_out, w_conv_out, w_out,
              norm2_gain, w_peer_q, peer_subkeys, peer_u, peer_v):
    yp, ys = x_prompt, x_sample
    p_kv = [[], [], []]
    s_kv = [[], [], []]
    p_conv, s_conv = [], []
    for layer in range(DEPTH):
        params = (norm1_gain[layer], w_in[layer], q_norm_gain[layer], k_norm_gain[layer], conv_w[layer],
                  w_attn_out[layer], w_conv_out[layer], w_out[layer], norm2_gain[layer],
                  w_peer_q[layer], peer_subkeys[layer], peer_u[layer], peer_v[layer])
        yp, kvp, cp = hybrid_layer(yp, None, None, *params)
        caches = (cache_kv_w128[layer], cache_kv_w512[layer], cache_kv_w2048[layer])
        ys, kvs, cs = hybrid_layer(ys, caches, state_conv[layer], *params)
        for g in range(N_GROUPS):
            p_kv[g].append(kvp[g])
            s_kv[g].append(kvs[g])
        p_conv.append(cp)
        s_conv.append(cs)
    return (yp, ys,
            jnp.stack(p_kv[0]), jnp.stack(p_kv[1]), jnp.stack(p_kv[2]), jnp.stack(p_conv),
            jnp.stack(s_kv[0]), jnp.stack(s_kv[1]), jnp.stack(s_kv[2]), jnp.stack(s_conv))
```

```python
import functools
import math

import jax
import jax.numpy as jnp
from jax import lax
from jax.experimental import pallas as pl
from jax.experimental.pallas import tpu as pltpu

HEAD_DIM = 128
DILATED_GROUPS = ((128, 1), (512, 4), (2048, 16))
N_GROUPS = len(DILATED_GROUPS)
CONV_K = 3
PEER_N_KEYS = 128
PEER_HALF = 128
PEER_TOPK = 16
EPS = 1e-6

LANES = 128
SUBLANES = 8
VMEM_LIMIT_BYTES = 56 * 1024 * 1024
NEG_INF = float("-inf")

_BF16 = jnp.bfloat16
_F32 = jnp.float32


def _params(sem):
    return pltpu.CompilerParams(dimension_semantics=sem, vmem_limit_bytes=VMEM_LIMIT_BYTES)


def _dot(a, b, contract=(1, 0)):
    return lax.dot_general(a, b, (((contract[0],), (contract[1],)), ((), ())),
                           preferred_element_type=_F32)


def _pick_tile(n, cap, mult):
    best = None
    t = mult
    while t <= min(n, cap):
        if n % t == 0:
            best = t
        t += mult
    assert best is not None, (n, cap, mult)
    return best


def _norm_kernel(x_ref, g_ref, o_ref, *rest, transpose):
    x = x_ref[...]
    ms = jnp.mean(x * x, axis=-1, keepdims=True)
    y = x * lax.rsqrt(ms + EPS) * g_ref[...]
    o_ref[...] = y.astype(o_ref.dtype)
    if transpose:
        rest[0][...] = y.T.astype(rest[0].dtype)


def _rmsnorm(x, gain, tm, transpose=False):
    n, d = x.shape
    out_shape = [jax.ShapeDtypeStruct((n, d), _BF16)]
    out_specs = [pl.BlockSpec((tm, d), lambda i: (i, 0))]
    if transpose:
        out_shape.append(jax.ShapeDtypeStruct((d, n), _BF16))
        out_specs.append(pl.BlockSpec((d, tm), lambda i: (0, i)))
    res = pl.pallas_call(
        functools.partial(_norm_kernel, transpose=transpose),
        grid=(n // tm,),
        in_specs=[pl.BlockSpec((tm, d), lambda i: (i, 0)),
                  pl.BlockSpec((1, d), lambda i: (0, 0))],
        out_specs=out_specs,
        out_shape=out_shape,
        compiler_params=_params(("arbitrary",)),
        name="rmsnorm_t" if transpose else "rmsnorm",
    )(x, gain.reshape(1, d))
    return res if transpose else res[0]


def _mm_kernel(*refs, n_lhs, n_w, pairs, n_extra, n_out, epilogue, transpose_out):
    lhs = refs[:n_lhs]
    ws = refs[n_lhs:n_lhs + n_w]
    extra = refs[n_lhs + n_w:n_lhs + n_w + n_extra]
    outs = refs[n_lhs + n_w + n_extra:n_lhs + n_w + n_extra + n_out]
    scr = refs[n_lhs + n_w + n_extra + n_out:]

    @pl.when(pl.program_id(1) == 0)
    def _():
        for w_ref, s_ref in zip(ws, scr):
            s_ref[...] = w_ref[...].astype(_BF16)

    accs = [_dot(lhs[a][...], scr[b][...]) for a, b in pairs]
    res = epilogue(accs, [e[...] for e in extra])
    for o_ref, r in zip(outs, res):
        if transpose_out:
            r = r.T
        o_ref[...] = r.astype(o_ref.dtype)


def _matmul(lhs_list, w_list, pairs, epilogue, out_dtypes, n_cols, tm, tn,
            extras=(), transpose_out=False, name="matmul"):
    n = lhs_list[0].shape[0]
    grid = (n_cols // tn, n // tm)
    in_specs, args = [], []
    for a in lhs_list:
        in_specs.append(pl.BlockSpec((tm, a.shape[1]), lambda j, i: (i, 0)))
        args.append(a)
    scratch = []
    for w, off in w_list:
        assert off % tn == 0
        in_specs.append(pl.BlockSpec((w.shape[0], tn), lambda j, i, o=off // tn: (0, o + j)))
        args.append(w)
        scratch.append(pltpu.VMEM((w.shape[0], tn), _BF16))
    for arr, kind, off in extras:
        assert off % tn == 0
        if kind == "tile":
            in_specs.append(pl.BlockSpec((tm, tn), lambda j, i, o=off // tn: (i, o + j)))
        else:
            in_specs.append(pl.BlockSpec((1, tn), lambda j, i, o=off // tn: (0, o + j)))
        args.append(arr)
    if transpose_out:
        out_shape = [jax.ShapeDtypeStruct((n_cols, n), dt) for dt in out_dtypes]
        out_specs = [pl.BlockSpec((tn, tm), lambda j, i: (j, i)) for _ in out_dtypes]
    else:
        out_shape = [jax.ShapeDtypeStruct((n, n_cols), dt) for dt in out_dtypes]
        out_specs = [pl.BlockSpec((tm, tn), lambda j, i: (i, j)) for _ in out_dtypes]
    kern = functools.partial(
        _mm_kernel, n_lhs=len(lhs_list), n_w=len(w_list), pairs=tuple(pairs),
        n_extra=len(extras), n_out=len(out_dtypes), epilogue=epilogue,
        transpose_out=transpose_out)
    return pl.pallas_call(
        kern, grid=grid, in_specs=in_specs, out_specs=out_specs, out_shape=out_shape,
        scratch_shapes=scratch, compiler_params=_params(("arbitrary", "arbitrary")),
        name=name,
    )(*args)


def _ep_identity(accs, extras):
    return [accs[0]]


def _ep_headnorm(accs, extras):
    acc, gain = accs[0], extras[0]
    cols = []
    for c in range(acc.shape[1] // HEAD_DIM):
        blk = acc[:, c * HEAD_DIM:(c + 1) * HEAD_DIM]
        ms = jnp.mean(blk * blk, axis=-1, keepdims=True)
        cols.append(blk * lax.rsqrt(ms + EPS) * gain[:, c * HEAD_DIM:(c + 1) * HEAD_DIM])
    return [jnp.concatenate(cols, axis=1) if len(cols) > 1 else cols[0]]


def _ep_product(accs, extras):
    return [accs[0] * accs[1]]


def _ep_sigmoid(accs, extras):
    return [jax.nn.sigmoid(accs[0])]


def _ep_merge(accs, extras):
    return [extras[0].astype(_F32) * accs[0] + extras[1].astype(_F32) * accs[1]]


def _ep_residual(accs, extras):
    return [extras[0] + accs[0]]


def _attn_prompt_kernel(q_ref, k_ref, v_ref, o_ref, m_scr, l_scr, acc_scr, *, seq, groups):
    step = pl.program_id(2)
    scale = HEAD_DIM ** -0.5
    bq = LANES
    n_steps = len(groups)

    def run_group(window, dil, first, last):
        band = window // dil
        sub_len = seq // dil
        nblk = sub_len // bq
        nkeys = band + bq
        assert sub_len >= nkeys and band == bq

        def body(idx, carry):
            r = idx // nblk
            blk = idx % nblk
            s0 = blk * bq
            k0 = jnp.maximum(s0 - band, 0)
            if dil == 1:
                q_rows = pl.ds(pl.multiple_of(s0, bq), bq)
                k_rows = pl.ds(pl.multiple_of(k0, bq), nkeys)
            else:
                q_rows = pl.ds(r + dil * s0, bq, stride=dil)
                k_rows = pl.ds(r + dil * k0, nkeys, stride=dil)
            q = q_ref[q_rows, :].astype(_BF16)
            k = k_ref[k_rows, :].astype(_BF16)
            v = v_ref[k_rows, :].astype(_BF16)
            s = _dot(q, k, (1, 1)) * scale
            qi = s0 + lax.broadcasted_iota(jnp.int32, (bq, nkeys), 0)
            ki = k0 + lax.broadcasted_iota(jnp.int32, (bq, nkeys), 1)
            diff = qi - ki
            s = jnp.where((diff >= 0) & (diff <= band), s, NEG_INF)
            m_b = jnp.max(s, axis=-1, keepdims=True)
            p = jnp.exp(s - m_b)
            l_b = jnp.sum(p, axis=-1, keepdims=True)
            acc_b = _dot(p.astype(_BF16), v)
            m_b = jnp.broadcast_to(m_b, (bq, LANES))
            l_b = jnp.broadcast_to(l_b, (bq, LANES))
            if not first:
                m_old = m_scr[q_rows, :]
                m_new = jnp.maximum(m_old, m_b)
                a_old = jnp.exp(m_old - m_new)
                a_b = jnp.exp(m_b - m_new)
                l_b = a_old * l_scr[q_rows, :] + a_b * l_b
                acc_b = a_old * acc_scr[q_rows, :] + a_b * acc_b
                m_b = m_new
            if last:
                o_ref[q_rows, :] = (acc_b / l_b).astype(o_ref.dtype)
            else:
                m_scr[q_rows, :] = m_b
                l_scr[q_rows, :] = l_b
                acc_scr[q_rows, :] = acc_b
            return carry

        lax.fori_loop(0, dil * nblk, body, 0)

    order = sorted(range(n_steps), key=lambda g: -groups[g][1])
    assert groups[order[-1]][1] == 1
    for pos, g in enumerate(order):
        @pl.when(step == pos)
        def _(g=g, pos=pos):
            run_group(groups[g][0], groups[g][1], pos == 0, pos == n_steps - 1)


def _attn_prompt(qk, v, batch, seq, heads):
    n_steps = N_GROUPS
    order = sorted(range(N_GROUPS), key=lambda g: -DILATED_GROUPS[g][1])
    gh = N_GROUPS * heads

    def col(step, h):
        g = jnp.int32(order[0])
        for pos in range(1, n_steps):
            g = jnp.where(step == pos, order[pos], g)
        return g * heads + h

    kern = functools.partial(_attn_prompt_kernel, seq=seq, groups=DILATED_GROUPS)
    return pl.pallas_call(
        kern,
        grid=(batch, heads, n_steps),
        in_specs=[
            pl.BlockSpec((seq, HEAD_DIM), lambda b, h, s: (b, col(s, h))),
            pl.BlockSpec((seq, HEAD_DIM), lambda b, h, s: (b, gh + col(s, h))),
            pl.BlockSpec((seq, HEAD_DIM), lambda b, h, s: (b, col(s, h))),
        ],
        out_specs=pl.BlockSpec((seq, HEAD_DIM), lambda b, h, s: (b, h)),
        out_shape=jax.ShapeDtypeStruct((batch * seq, heads * HEAD_DIM), _BF16),
        scratch_shapes=[pltpu.VMEM((seq, LANES), _F32)] * 3,
        compiler_params=_params(("arbitrary", "arbitrary", "arbitrary")),
        name="attn_prompt",
    )(qk, qk, v)


def _attn_sample_kernel(qk_ref, v_ref, c0_ref, c1_ref, c2_ref, o_ref, *, heads, bb):
    scale = HEAD_DIM ** -0.5
    width = heads * HEAD_DIM
    caches = (c0_ref, c1_ref, c2_ref)
    kbase = N_GROUPS * width
    for bi in range(bb):
        q_row = qk_ref[0, bi:bi + 1, :]
        v_row = v_ref[0, bi:bi + 1, :]
        outs = []
        for h in range(heads):
            s_old, s_new, vals = [], [], []
            for g in range(N_GROUPS):
                lo = g * width + h * HEAD_DIM
                qh = q_row[:, lo:lo + HEAD_DIM]
                kn = q_row[:, kbase + lo:kbase + lo + HEAD_DIM]
                kc = caches[g][bi, :, h * HEAD_DIM:(h + 1) * HEAD_DIM]
                vc = caches[g][bi, :, width + h * HEAD_DIM:width + (h + 1) * HEAD_DIM]
                s_old.append(jnp.sum(kc * qh, axis=-1, keepdims=True) * scale)
                s_new.append(jnp.sum(kn * qh, axis=-1, keepdims=True) * scale)
                vals.append((vc, v_row[:, lo:lo + HEAD_DIM]))
            m = s_new[0]
            for g in range(N_GROUPS):
                m = jnp.maximum(m, jnp.maximum(s_new[g], jnp.max(s_old[g], axis=0, keepdims=True)))
            l = jnp.zeros((1, 1), _F32)
            acc = jnp.zeros((1, HEAD_DIM), _F32)
            for g in range(N_GROUPS):
                p_old = jnp.exp(s_old[g] - m)
                p_new = jnp.exp(s_new[g] - m)
                l = l + jnp.sum(p_old, axis=0, keepdims=True) + p_new
                acc = acc + jnp.sum(p_old * vals[g][0], axis=0, keepdims=True) + p_new * vals[g][1]
            outs.append(acc / l)
        o_ref[0, bi:bi + 1, :] = jnp.concatenate(outs, axis=1).astype(o_ref.dtype)


def _attn_sample(qk_s, v_s, caches, heads):
    db = qk_s.shape[0]
    width = heads * HEAD_DIM
    bb = 4
    assert db % bb == 0
    cache_args, cache_specs = [], []
    for (window, dil), c in zip(DILATED_GROUPS, caches):
        past = c.shape[1]
        assert past == window and window // dil == PEER_N_KEYS
        cache_args.append(c.reshape(db, past // dil, dil * 2 * width))
        cache_specs.append(pl.BlockSpec((bb, past // dil, 2 * width), lambda i: (i, 0, 0)))
    kern = functools.partial(_attn_sample_kernel, heads=heads, bb=bb)
    out = pl.pallas_call(
        kern,
        grid=(db // bb,),
        in_specs=[pl.BlockSpec((1, bb, qk_s.shape[1]), lambda i: (i, 0, 0)),
                  pl.BlockSpec((1, bb, v_s.shape[1]), lambda i: (i, 0, 0))] + cache_specs,
        out_specs=pl.BlockSpec((1, bb, width), lambda i: (i, 0, 0)),
        out_shape=jax.ShapeDtypeStruct((db // bb, bb, width), _F32),
        compiler_params=_params(("arbitrary",)),
        name="attn_sample",
    )(qk_s.reshape(db // bb, bb, -1), v_s.reshape(db // bb, bb, -1), *cache_args)
    return out.reshape(db, width)


def _conv_prompt_kernel(u_ref, halo_ref, gb_ref, w_ref, o_ref, *, seq, tm):
    i = pl.program_id(0)
    u = u_ref[...]
    halo = halo_ref[...]
    halo = jnp.where((i * tm) % seq == 0, 0.0, halo)
    h6 = halo[SUBLANES - 2:SUBLANES - 1, :]
    h7 = halo[SUBLANES - 1:SUBLANES, :]
    rows = lax.broadcasted_iota(jnp.int32, u.shape, 0)
    um1 = jnp.where(rows == 0, h7, pltpu.roll(u, 1, 0))
    um2 = jnp.where(rows == 0, h6, jnp.where(rows == 1, h7, pltpu.roll(u, 2, 0)))
    w = w_ref[...]
    y = w[0:1, :] * um2 + w[1:2, :] * um1 + w[2:3, :] * u
    o_ref[...] = (gb_ref[...].astype(_F32) * y).astype(o_ref.dtype)


def _conv_prompt(u, gate_b, conv_w, n_rows, seq, tm, tc):
    c = u.shape[1]
    assert seq % tm == 0 and tm % SUBLANES == 0
    hb = tm // SUBLANES
    kern = functools.partial(_conv_prompt_kernel, seq=seq, tm=tm)
    return pl.pallas_call(
        kern,
        grid=(n_rows // tm, c // tc),
        in_specs=[pl.BlockSpec((tm, tc), lambda i, j: (i, j)),
                  pl.BlockSpec((SUBLANES, tc), lambda i, j: (jnp.maximum(i * hb - 1, 0), j)),
                  pl.BlockSpec((tm, tc), lambda i, j: (i, j)),
                  pl.BlockSpec((CONV_K, tc), lambda i, j: (0, j))],
        out_specs=pl.BlockSpec((tm, tc), lambda i, j: (i, j)),
        out_shape=jax.ShapeDtypeStruct((n_rows, c), _BF16),
        compiler_params=_params(("arbitrary", "arbitrary")),
        name="conv_prompt",
    )(u, u, gate_b, conv_w)


def _conv_sample_kernel(u_ref, p0_ref, p1_ref, gb_ref, w_ref, o_ref):
    w = w_ref[...]
    y = w[0:1, :] * p0_ref[...] + w[1:2, :] * p1_ref[...] + w[2:3, :] * u_ref[...]
    o_ref[...] = (gb_ref[...].astype(_F32) * y).astype(o_ref.dtype)


def _conv_sample(u_s, prev0, prev1, gate_b_s, conv_w, tc):
    db, c = u_s.shape
    spec = pl.BlockSpec((db, tc), lambda j: (0, j))
    return pl.pallas_call(
        _conv_sample_kernel,
        grid=(c // tc,),
        in_specs=[spec, spec, spec, spec, pl.BlockSpec((CONV_K, tc), lambda j: (0, j))],
        out_specs=spec,
        out_shape=jax.ShapeDtypeStruct((db, c), _BF16),
        compiler_params=_params(("arbitrary",)),
        name="conv_sample",
    )(u_s, prev0, prev1, gate_b_s, conv_w)


def _topk_desc(s, vals_ref):
    n_keys = s.shape[0]
    key_id = lax.broadcasted_iota(jnp.int32, s.shape, 0).astype(_F32)

    def body(a, carry):
        s, rank = carry
        m = jnp.max(s, axis=0, keepdims=True)
        first = jnp.min(jnp.where(s == m, key_id, float(n_keys)), axis=0, keepdims=True)
        sel = key_id == first
        vals_ref[pl.ds(a, 1), :] = m
        rank = jnp.where(sel, a.astype(_F32), rank)
        s = jnp.where(sel, NEG_INF, s)
        return s, rank

    _, rank = lax.fori_loop(0, PEER_TOPK, body, (s, jnp.full(s.shape, float(PEER_TOPK), _F32)))
    return rank


def _routing_kernel(qpT_ref, sk_ref, nb1_ref, a1_ref, r2_ref, a2_ref, v1_scr, v2_scr, *, p_heads):
    k = PEER_TOPK
    tm = qpT_ref.shape[1]
    row8 = lax.broadcasted_iota(jnp.int32, (SUBLANES, tm), 0)
    row16 = lax.broadcasted_iota(jnp.int32, (2 * SUBLANES, tm), 0)

    pos_parts = [(row16 * k).astype(_F32)]
    valid_parts = [row16 >= 0]
    for b in range(1, SUBLANES):
        pos_parts.append((row8 * k + b).astype(_F32))
        valid_parts.append((row8 + 1) * (b + 1) <= k)
    pos_parts.append((row8 + SUBLANES).astype(_F32))
    valid_parts.append(row8 >= 0)
    pos = jnp.concatenate(pos_parts, axis=0)
    valid = jnp.concatenate(valid_parts, axis=0)

    def per_head(h, carry):
        q1 = qpT_ref[pl.ds(pl.multiple_of(h * 2 * PEER_HALF, PEER_HALF), PEER_HALF), :]
        q2 = qpT_ref[pl.ds(pl.multiple_of(h * 2 * PEER_HALF + PEER_HALF, PEER_HALF), PEER_HALF), :]
        s1 = _dot(sk_ref[h, 0].astype(_BF16), q1.astype(_BF16))
        s2 = _dot(sk_ref[h, 1].astype(_BF16), q2.astype(_BF16))
        r1 = _topk_desc(s1, v1_scr)
        r2 = _topk_desc(s2, v2_scr)
        v1 = v1_scr[...]
        v2 = v2_scr[...]

        parts = [v1 + v2[0:1, :]]
        for b in range(1, SUBLANES):
            parts.append(v1[0:SUBLANES, :] + v2[b:b + 1, :])
        parts.append(v1[0:1, :] + v2[SUBLANES:2 * SUBLANES, :])
        cand = jnp.where(valid, jnp.concatenate(parts, axis=0), NEG_INF)
        cmax = v1[0:1, :] + v2[0:1, :]

        def pick(_, carry):
            cand, chosen, z = carry
            m = jnp.max(cand, axis=0, keepdims=True)
            first = jnp.min(jnp.where(cand == m, pos, float(k * k)), axis=0, keepdims=True)
            sel = pos == first
            chosen = jnp.where(sel, 1.0, chosen)
            cand = jnp.where(sel, NEG_INF, cand)
            return cand, chosen, z + jnp.exp(m - cmax)

        _, chosen, z = lax.fori_loop(
            0, k, pick, (cand, jnp.zeros(cand.shape, _F32), jnp.zeros((1, tm), _F32)))

        nb_lo = chosen[0:SUBLANES, :]
        for b in range(1, SUBLANES):
            nb_lo = nb_lo + chosen[2 * SUBLANES + (b - 1) * SUBLANES:2 * SUBLANES + b * SUBLANES, :]
        tail = jnp.sum(chosen[2 * SUBLANES + 7 * SUBLANES:, :], axis=0, keepdims=True)
        nb_lo = nb_lo + jnp.where(row8 == 0, tail, 0.0)
        nb = jnp.concatenate([nb_lo, chosen[SUBLANES:2 * SUBLANES, :]], axis=0)

        nb1 = jnp.zeros(r1.shape, _F32)
        for a in range(k):
            nb1 = jnp.where(r1 == float(a), nb[a:a + 1, :], nb1)
        nb1_ref[h] = nb1
        a1_ref[h] = jnp.exp(s1 - v1[0:1, :])
        r2_ref[h] = r2
        a2_ref[h] = jnp.exp(s2 - v2[0:1, :]) / z
        return carry

    lax.fori_loop(0, p_heads, per_head, 0)


def _routing(qpT, subkeys, tmr):
    p_heads = subkeys.shape[0]
    n = qpT.shape[1]
    assert subkeys.shape[1:] == (2, PEER_N_KEYS, PEER_HALF)
    out = jax.ShapeDtypeStruct((p_heads, PEER_N_KEYS, n), _F32)
    spec = pl.BlockSpec((p_heads, PEER_N_KEYS, tmr), lambda i: (0, 0, i))
    return pl.pallas_call(
        functools.partial(_routing_kernel, p_heads=p_heads),
        grid=(n // tmr,),
        in_specs=[pl.BlockSpec((qpT.shape[0], tmr), lambda i: (0, i)),
                  pl.BlockSpec(subkeys.shape, lambda i: (0, 0, 0, 0))],
        out_specs=[spec] * 4,
        out_shape=[out] * 4,
        scratch_shapes=[pltpu.VMEM((PEER_TOPK, tmr), _F32)] * 2,
        compiler_params=_params(("arbitrary",)),
        name="peer_routing",
    )(qpT, subkeys)


def _peer_kernel(hT_ref, u_ref, v_ref, nb1_ref, a1_ref, r2_ref, a2_ref, o_ref, acc_scr, *, p_heads, te):
    j = pl.program_id(1)
    n_i1 = te // PEER_N_KEYS

    @pl.when(j == 0)
    def _():
        acc_scr[...] = jnp.zeros_like(acc_scr)

    act = _dot(u_ref[...], hT_ref[...])
    act = 0.5 * act * (1.0 + lax.erf(act * (0.5 ** 0.5)))
    pieces = []
    for c in range(n_i1):
        g = jnp.zeros((PEER_N_KEYS, act.shape[1]), _F32)
        for h in range(p_heads):
            nb = nb1_ref[c, h:h + 1, :]
            a1 = a1_ref[c, h:h + 1, :]
            g = g + jnp.where(r2_ref[h] < nb, a2_ref[h] * a1, 0.0)
        pieces.append((g * act[c * PEER_N_KEYS:(c + 1) * PEER_N_KEYS, :]).astype(_BF16))
    ga = jnp.concatenate(pieces, axis=0) if n_i1 > 1 else pieces[0]
    acc_scr[...] += _dot(ga, v_ref[...], (0, 0))

    @pl.when(j == pl.num_programs(1) - 1)
    def _():
        o_ref[...] = acc_scr[...].astype(o_ref.dtype)


def _peer(hT, u_bf, v_bf, routing, tm, te):
    d, n = hT.shape
    n_exp = u_bf.shape[0]
    nb1, a1, r2, a2 = routing
    p_heads = r2.shape[0]
    n_i1 = te // PEER_N_KEYS
    nb1 = jnp.transpose(nb1, (1, 0, 2))
    a1 = jnp.transpose(a1, (1, 0, 2))
    k1spec = pl.BlockSpec((n_i1, p_heads, tm), lambda i, j: (j, 0, i))
    k2spec = pl.BlockSpec((p_heads, PEER_N_KEYS, tm), lambda i, j: (0, 0, i),
                          pipeline_mode=pl.Buffered(1))
    kern = functools.partial(_peer_kernel, p_heads=p_heads, te=te)
    return pl.pallas_call(
        kern,
        grid=(n // tm, n_exp // te),
        in_specs=[pl.BlockSpec((d, tm), lambda i, j: (0, i), pipeline_mode=pl.Buffered(1)),
                  pl.BlockSpec((te, d), lambda i, j: (j, 0)),
                  pl.BlockSpec((te, d), lambda i, j: (j, 0)),
                  k1spec, k1spec, k2spec, k2spec],
        out_specs=pl.BlockSpec((tm, d), lambda i, j: (i, 0)),
        out_shape=jax.ShapeDtypeStruct((n, d), _BF16),
        scratch_shapes=[pltpu.VMEM((tm, d), _F32)],
        compiler_params=_params(("arbitrary", "arbitrary")),
        name="peer_experts",
    )(hT, u_bf, v_bf, nb1, a1, r2, a2)


def _layer(x_prompt, x_sample, caches, state_conv, norm1_gain, w_in, q_norm_gain, k_norm_gain,
           conv_w, w_attn_out, w_conv_out, w_out, norm2_gain, w_peer_q, peer_subkeys, peer_u, peer_v):
    batch, seq, d = x_prompt.shape
    db, dec_seq, _ = x_sample.shape
    assert dec_seq == 1
    width = w_attn_out.shape[0]
    heads = width // HEAD_DIM
    attn_w = N_GROUPS * width
    c = conv_w.shape[1]
    n_p = batch * seq
    n = n_p + db
    assert w_in.shape[1] == 3 * attn_w + 3 * c + 2 * d

    tm = _pick_tile(n, 640, LANES)
    tn = 512
    col_q, col_k, col_v = 0, attn_w, 2 * attn_w
    col_ci = 3 * attn_w
    col_gb, col_gc = col_ci + c, col_ci + 2 * c
    col_ga = col_ci + 3 * c

    x_all = jnp.concatenate([x_prompt.reshape(n_p, d), x_sample.reshape(db, d)], axis=0)
    xn = _rmsnorm(x_all, norm1_gain, tm)

    qk_gain = jnp.concatenate([jnp.tile(q_norm_gain, N_GROUPS * heads),
                               jnp.tile(k_norm_gain, N_GROUPS * heads)]).reshape(1, 2 * attn_w)
    (qk,) = _matmul([xn], [(w_in, col_q)], [(0, 0)], _ep_headnorm, [_F32], 2 * attn_w, tm, tn,
                    extras=[(qk_gain, "row", 0)], name="proj_qk")
    (v,) = _matmul([xn], [(w_in, col_v)], [(0, 0)], _ep_identity, [_F32], attn_w, tm, tn,
                   name="proj_v")
    (u,) = _matmul([xn], [(w_in, col_ci), (w_in, col_gc)], [(0, 0), (0, 1)], _ep_product,
                   [_F32], c, tm, 256, name="proj_u")
    (gate_b,) = _matmul([xn], [(w_in, col_gb)], [(0, 0)], _ep_identity, [_BF16], c, tm, tn,
                        name="proj_gate_b")
    (sig,) = _matmul([xn], [(w_in, col_ga)], [(0, 0)], _ep_sigmoid, [_BF16], 2 * d, tm, tn,
                     name="proj_sigmoid_gates")

    attn_p = _attn_prompt(qk, v, batch, seq, heads)
    attn_s = _attn_sample(qk[n_p:], v[n_p:], caches, heads)
    attn = jnp.concatenate([attn_p, attn_s.astype(_BF16)], axis=0)

    tcv = _pick_tile(c, 1024, LANES)
    cb_p = _conv_prompt(u, gate_b, conv_w, n_p, seq, _pick_tile(seq, 512, SUBLANES), tcv)
    u_s = u[n_p:]
    cb_s = _conv_sample(u_s, state_conv[:, 0], state_conv[:, 1], gate_b[n_p:], conv_w, tcv)
    cb = jnp.concatenate([cb_p, cb_s], axis=0)

    (merged,) = _matmul([attn, cb], [(w_attn_out, 0), (w_conv_out, 0)], [(0, 0), (1, 1)],
                        _ep_merge, [_BF16], d, tm, tn,
                        extras=[(sig, "tile", 0), (sig, "tile", d)], name="merge")
    (x1,) = _matmul([merged], [(w_out, 0)], [(0, 0)], _ep_residual, [_F32], d, tm, tn,
                    extras=[(x_all, "tile", 0)], name="out_proj")

    hn, hnT = _rmsnorm(x1, norm2_gain, tm, transpose=True)
    (qpT,) = _matmul([hn], [(w_peer_q, 0)], [(0, 0)], _ep_identity, [_F32], w_peer_q.shape[1],
                     tm, tn, transpose_out=True, name="peer_query")
    routing = _routing(qpT, peer_subkeys, LANES)
    te = 4 * PEER_N_KEYS
    peer_out = _peer(hnT, peer_u.astype(_BF16), peer_v.astype(_BF16), routing, tm, te)
    y = x1 + peer_out.astype(_F32)

    y_p = y[:n_p].reshape(batch, seq, d)
    y_s = y[n_p:].reshape(db, 1, d)
    kv_p, kv_s = [], []
    for g, (window, _) in enumerate(DILATED_GROUPS):
        kg = qk[:, attn_w + g * width:attn_w + (g + 1) * width]
        vg = v[:, g * width:(g + 1) * width]
        keep = min(window, seq)
        kp = kg[:n_p].reshape(batch, seq, heads, HEAD_DIM)[:, seq - keep:]
        vp = vg[:n_p].reshape(batch, seq, heads, HEAD_DIM)[:, seq - keep:]
        kv_p.append(jnp.stack([kp, vp], axis=2))
        ks = kg[n_p:].reshape(db, 1, heads, HEAD_DIM)
        vs = vg[n_p:].reshape(db, 1, heads, HEAD_DIM)
        kv_s.append(jnp.stack([ks, vs], axis=2))
    conv_p = u[:n_p].reshape(batch, seq, c)[:, seq - (CONV_K - 1):]
    conv_s = jnp.stack([state_conv[:, 1], u_s], axis=1)
    return y_p, y_s, kv_p, conv_p, kv_s, conv_s


def kernel(x_prompt, x_sample, cache_kv_w128, cache_kv_w512, cache_kv_w2048, state_conv, norm1_gain,
           w_in, q_norm_gain, k_norm_gain, conv_w, w_attn_out, w_conv_out, w_out, norm2_gain,
           w_peer_q, peer_subkeys, peer_u, peer_v):
    depth = w_in.shape[0]
    yp, ys = x_prompt, x_sample
    kv_p = [[] for _ in range(N_GROUPS)]
    kv_s = [[] for _ in range(N_GROUPS)]
    conv_p, conv_s = [], []
    for layer in range(depth):
        caches = (cache_kv_w128[layer], cache_kv_w512[layer], cache_kv_w2048[layer])
        yp, ys, kvp, cp, kvs, cs = _layer(
            yp, ys, caches, state_conv[layer], norm1_gain[layer], w_in[layer], q_norm_gain[layer],
            k_norm_gain[layer], conv_w[layer], w_attn_out[layer], w_conv_out[layer], w_out[layer],
            norm2_gain[layer], w_peer_q[layer], peer_subkeys[layer], peer_u[layer], peer_v[layer])
        for g in range(N_GROUPS):
            kv_p[g].append(kvp[g])
            kv_s[g].append(kvs[g])
        conv_p.append(cp)
        conv_s.append(cs)
    return (yp, ys,
            jnp.stack(kv_p[0]), jnp.stack(kv_p[1]), jnp.stack(kv_p[2]), jnp.stack(conv_p),
            jnp.stack(kv_s[0]), jnp.stack(kv_s[1]), jnp.stack(kv_s[2]), jnp.stack(conv_s))
```

```python
import functools
import math

import jax
import jax.numpy as jnp
from jax import lax
from jax.experimental import pallas as pl
from jax.experimental.pallas import tpu as pltpu

HEAD_DIM = 128
DILATED_GROUPS = ((128, 1), (512, 4), (2048, 16))
N_GROUPS = len(DILATED_GROUPS)
CONV_K = 3
PEER_N_KEYS = 128
PEER_HALF = 128
PEER_TOPK = 16
EPS = 1e-6

LANES = 128
SUBLANES = 8
VMEM_LIMIT_BYTES = 56 * 1024 * 1024
NEG_INF = float("-inf")

_BF16 = jnp.bfloat16
_F32 = jnp.float32


def _params(sem):
    return pltpu.CompilerParams(dimension_semantics=sem, vmem_limit_bytes=VMEM_LIMIT_BYTES)


def _dot(a, b, contract=(1, 0)):
    return lax.dot_general(a, b, (((contract[0],), (contract[1],)), ((), ())),
                           preferred_element_type=_F32)


def _pick_tile(n, cap, mult):
    best = None
    t = mult
    while t <= min(n, cap):
        if n % t == 0:
            best = t
        t += mult
    assert best is not None, (n, cap, mult)
    return best


def _norm_kernel(x_ref, g_ref, o_ref, *rest, transpose):
    x = x_ref[...]
    ms = jnp.mean(x * x, axis=-1, keepdims=True)
    y = x * lax.rsqrt(ms + EPS) * g_ref[...]
    o_ref[...] = y.astype(o_ref.dtype)
    if transpose:
        rest[0][...] = y.T.astype(rest[0].dtype)


def _rmsnorm(x, gain, tm, transpose=False):
    n, d = x.shape
    out_shape = [jax.ShapeDtypeStruct((n, d), _BF16)]
    out_specs = [pl.BlockSpec((tm, d), lambda i: (i, 0))]
    if transpose:
        out_shape.append(jax.ShapeDtypeStruct((d, n), _BF16))
        out_specs.append(pl.BlockSpec((d, tm), lambda i: (0, i)))
    res = pl.pallas_call(
        functools.partial(_norm_kernel, transpose=transpose),
        grid=(n // tm,),
        in_specs=[pl.BlockSpec((tm, d), lambda i: (i, 0)),
                  pl.BlockSpec((1, d), lambda i: (0, 0))],
        out_specs=out_specs,
        out_shape=out_shape,
        compiler_params=_params(("arbitrary",)),
        name="rmsnorm_t" if transpose else "rmsnorm",
    )(x, gain.reshape(1, d))
    return res if transpose else res[0]


def _mm_kernel(*refs, n_lhs, n_w, pairs, n_extra, n_out, epilogue, transpose_out):
    lhs = refs[:n_lhs]
    ws = refs[n_lhs:n_lhs + n_w]
    extra = refs[n_lhs + n_w:n_lhs + n_w + n_extra]
    outs = refs[n_lhs + n_w + n_extra:n_lhs + n_w + n_extra + n_out]
    scr = refs[n_lhs + n_w + n_extra + n_out:]

    @pl.when(pl.program_id(1) == 0)
    def _():
        for w_ref, s_ref in zip(ws, scr):
            s_ref[...] = w_ref[...].astype(_BF16)

    accs = [_dot(lhs[a][...], scr[b][...]) for a, b in pairs]
    res = epilogue(accs, [e[...] for e in extra])
    for o_ref, r in zip(outs, res):
        if transpose_out:
            r = r.T
        o_ref[...] = r.astype(o_ref.dtype)


def _matmul(lhs_list, w_list, pairs, epilogue, out_dtypes, n_cols, tm, tn,
            extras=(), transpose_out=False, name="matmul"):
    n = lhs_list[0].shape[0]
    grid = (n_cols // tn, n // tm)
    in_specs, args = [], []
    for a in lhs_list:
        in_specs.append(pl.BlockSpec((tm, a.shape[1]), lambda j, i: (i, 0)))
        args.append(a)
    scratch = []
    for w, off in w_list:
        assert off % tn == 0
        in_specs.append(pl.BlockSpec((w.shape[0], tn), lambda j, i, o=off // tn: (0, o + j)))
        args.append(w)
        scratch.append(pltpu.VMEM((w.shape[0], tn), _BF16))
    for arr, kind, off in extras:
        assert off % tn == 0
        if kind == "tile":
            in_specs.append(pl.BlockSpec((tm, tn), lambda j, i, o=off // tn: (i, o + j)))
        else:
            in_specs.append(pl.BlockSpec((1, tn), lambda j, i, o=off // tn: (0, o + j)))
        args.append(arr)
    if transpose_out:
        out_shape = [jax.ShapeDtypeStruct((n_cols, n), dt) for dt in out_dtypes]
        out_specs = [pl.BlockSpec((tn, tm), lambda j, i: (j, i)) for _ in out_dtypes]
    else:
        out_shape = [jax.ShapeDtypeStruct((n, n_cols), dt) for dt in out_dtypes]
        out_specs = [pl.BlockSpec((tm, tn), lambda j, i: (i, j)) for _ in out_dtypes]
    kern = functools.partial(
        _mm_kernel, n_lhs=len(lhs_list), n_w=len(w_list), pairs=tuple(pairs),
        n_extra=len(extras), n_out=len(out_dtypes), epilogue=epilogue,
        transpose_out=transpose_out)
    return pl.pallas_call(
        kern, grid=grid, in_specs=in_specs, out_specs=out_specs, out_shape=out_shape,
        scratch_shapes=scratch, compiler_params=_params(("arbitrary", "arbitrary")),
        name=name,
    )(*args)


def _ep_identity(accs, extras):
    return [accs[0]]


def _ep_headnorm(accs, extras):
    acc, gain = accs[0], extras[0]
    cols = []
    for c in range(acc.shape[1] // HEAD_DIM):
        blk = acc[:, c * HEAD_DIM:(c + 1) * HEAD_DIM]
        ms = jnp.mean(blk * blk, axis=-1, keepdims=True)
        cols.append(blk * lax.rsqrt(ms + EPS) * gain[:, c * HEAD_DIM:(c + 1) * HEAD_DIM])
    return [jnp.concatenate(cols, axis=1) if len(cols) > 1 else cols[0]]


def _ep_product(accs, extras):
    return [accs[0] * accs[1]]


def _ep_sigmoid(accs, extras):
    return [jax.nn.sigmoid(accs[0])]


def _ep_merge(accs, extras):
    return [extras[0].astype(_F32) * accs[0] + extras[1].astype(_F32) * accs[1]]


def _ep_residual(accs, extras):
    return [extras[0] + accs[0]]


def _attn_prompt_kernel(q_ref, k_ref, v_ref, o_ref, m_scr, l_scr, acc_scr, *, seq, groups):
    step = pl.program_id(2)
    scale = HEAD_DIM ** -0.5
    bq = LANES
    n_steps = len(groups)

    def run_group(window, dil, first, last):
        band = window // dil
        sub_len = seq // dil
        nblk = sub_len // bq
        nkeys = band + bq
        assert sub_len >= nkeys and band == bq

        def body(idx, carry):
            r = idx // nblk
            blk = idx % nblk
            s0 = blk * bq
            k0 = jnp.maximum(s0 - band, 0)
            if dil == 1:
                q_rows = pl.ds(pl.multiple_of(s0, bq), bq)
                k_rows = pl.ds(pl.multiple_of(k0, bq), nkeys)
            else:
                q_rows = pl.ds(r + dil * s0, bq, stride=dil)
                k_rows = pl.ds(r + dil * k0, nkeys, stride=dil)
            q = q_ref[q_rows, :].astype(_BF16)
            k = k_ref[k_rows, :].astype(_BF16)
            v = v_ref[k_rows, :].astype(_BF16)
            s = _dot(q, k, (1, 1)) * scale
            qi = s0 + lax.broadcasted_iota(jnp.int32, (bq, nkeys), 0)
            ki = k0 + lax.broadcasted_iota(jnp.int32, (bq, nkeys), 1)
            diff = qi - ki
            s = jnp.where((diff >= 0) & (diff <= band), s, NEG_INF)
            m_b = jnp.max(s, axis=-1, keepdims=True)
            p = jnp.exp(s - m_b)
            l_b = jnp.sum(p, axis=-1, keepdims=True)
            acc_b = _dot(p.astype(_BF16), v)
            m_b = jnp.broadcast_to(m_b, (bq, LANES))
            l_b = jnp.broadcast_to(l_b, (bq, LANES))
            if not first:
                m_old = m_scr[q_rows, :]
                m_new = jnp.maximum(m_old, m_b)
                a_old = jnp.exp(m_old - m_new)
                a_b = jnp.exp(m_b - m_new)
                l_b = a_old * l_scr[q_rows, :] + a_b * l_b
                acc_b = a_old * acc_scr[q_rows, :] + a_b * acc_b
                m_b = m_new
            if last:
                o_ref[q_rows, :] = (acc_b / l_b).astype(o_ref.dtype)
            else:
                m_scr[q_rows, :] = m_b
                l_scr[q_rows, :] = l_b
                acc_scr[q_rows, :] = acc_b
            return carry

        lax.fori_loop(0, dil * nblk, body, 0, unroll=2)

    order = sorted(range(n_steps), key=lambda g: -groups[g][1])
    assert groups[order[-1]][1] == 1
    for pos, g in enumerate(order):
        @pl.when(step == pos)
        def _(g=g, pos=pos):
            run_group(groups[g][0], groups[g][1], pos == 0, pos == n_steps - 1)


def _attn_prompt(qk, v, batch, seq, heads):
    n_steps = N_GROUPS
    order = sorted(range(N_GROUPS), key=lambda g: -DILATED_GROUPS[g][1])
    gh = N_GROUPS * heads

    def col(step, h):
        g = jnp.int32(order[0])
        for pos in range(1, n_steps):
            g = jnp.where(step == pos, order[pos], g)
        return g * heads + h

    kern = functools.partial(_attn_prompt_kernel, seq=seq, groups=DILATED_GROUPS)
    return pl.pallas_call(
        kern,
        grid=(batch, heads, n_steps),
        in_specs=[
            pl.BlockSpec((seq, HEAD_DIM), lambda b, h, s: (b, col(s, h))),
            pl.BlockSpec((seq, HEAD_DIM), lambda b, h, s: (b, gh + col(s, h))),
            pl.BlockSpec((seq, HEAD_DIM), lambda b, h, s: (b, col(s, h))),
        ],
        out_specs=pl.BlockSpec((seq, HEAD_DIM), lambda b, h, s: (b, h)),
        out_shape=jax.ShapeDtypeStruct((batch * seq, heads * HEAD_DIM), _BF16),
        scratch_shapes=[pltpu.VMEM((seq, LANES), _F32)] * 3,
        compiler_params=_params(("arbitrary", "arbitrary", "arbitrary")),
        name="attn_prompt",
    )(qk, qk, v)


def _attn_sample_kernel(q_ref, v_ref, c0_ref, c1_ref, c2_ref, o_ref, *, heads, bb):
    scale = HEAD_DIM ** -0.5
    caches = (c0_ref, c1_ref, c2_ref)
    for bi in range(bb):
        s_old, s_new = [], []
        for g in range(N_GROUPS):
            q = q_ref[bi, g * heads:(g + 1) * heads, :]
            kn = q_ref[bi, (N_GROUPS + g) * heads:(N_GROUPS + g + 1) * heads, :]
            s_old.append(jnp.sum(caches[g][bi, :, 0] * q[None], axis=-1, keepdims=True) * scale)
            s_new.append(jnp.sum(kn * q, axis=-1, keepdims=True) * scale)
        m = s_new[0]
        for g in range(N_GROUPS):
            m = jnp.maximum(m, jnp.maximum(s_new[g], jnp.max(s_old[g], axis=0)))
        l = jnp.zeros((heads, 1), _F32)
        acc = jnp.zeros((heads, HEAD_DIM), _F32)
        for g in range(N_GROUPS):
            p_old = jnp.exp(s_old[g] - m[None])
            p_new = jnp.exp(s_new[g] - m)
            vn = v_ref[bi, g * heads:(g + 1) * heads, :]
            l = l + jnp.sum(p_old, axis=0) + p_new
            acc = acc + jnp.sum(p_old * caches[g][bi, :, 1], axis=0) + p_new * vn
        o_ref[bi] = (acc / l).astype(o_ref.dtype)


def _attn_sample(qk_s, v_s, caches, layer, heads):
    db = qk_s.shape[0]
    width = heads * HEAD_DIM
    bb = 4
    assert db % bb == 0
    cache_args, cache_specs = [], []
    for (window, dil), c in zip(DILATED_GROUPS, caches):
        depth, _, past = c.shape[:3]
        assert past == window
        taps = past // dil
        cache_args.append(c.reshape(depth, db, taps, dil, 2, heads, HEAD_DIM))
        cache_specs.append(pl.BlockSpec((None, bb, taps, None, 2, heads, HEAD_DIM),
                                        lambda i, layer=layer: (layer, i, 0, 0, 0, 0, 0)))
    kern = functools.partial(_attn_sample_kernel, heads=heads, bb=bb)
    out = pl.pallas_call(
        kern,
        grid=(db // bb,),
        in_specs=[pl.BlockSpec((bb, 2 * N_GROUPS * heads, HEAD_DIM), lambda i: (i, 0, 0)),
                  pl.BlockSpec((bb, N_GROUPS * heads, HEAD_DIM), lambda i: (i, 0, 0))] + cache_specs,
        out_specs=pl.BlockSpec((bb, heads, HEAD_DIM), lambda i: (i, 0, 0)),
        out_shape=jax.ShapeDtypeStruct((db, heads, HEAD_DIM), _F32),
        compiler_params=_params(("arbitrary",)),
        name="attn_sample",
    )(qk_s.reshape(db, -1, HEAD_DIM), v_s.reshape(db, -1, HEAD_DIM), *cache_args)
    return out.reshape(db, width)


def _conv_prompt_kernel(u_ref, halo_ref, gb_ref, w_ref, o_ref, *, seq, tm):
    i = pl.program_id(0)
    u = u_ref[...]
    halo = halo_ref[...]
    halo = jnp.where((i * tm) % seq == 0, 0.0, halo)
    h6 = halo[SUBLANES - 2:SUBLANES - 1, :]
    h7 = halo[SUBLANES - 1:SUBLANES, :]
    rows = lax.broadcasted_iota(jnp.int32, u.shape, 0)
    um1 = jnp.where(rows == 0, h7, pltpu.roll(u, 1, 0))
    um2 = jnp.where(rows == 0, h6, jnp.where(rows == 1, h7, pltpu.roll(u, 2, 0)))
    w = w_ref[...]
    y = w[0:1, :] * um2 + w[1:2, :] * um1 + w[2:3, :] * u
    o_ref[...] = (gb_ref[...].astype(_F32) * y).astype(o_ref.dtype)


def _conv_prompt(u, gate_b, conv_w, n_rows, seq, tm, tc):
    c = u.shape[1]
    assert seq % tm == 0 and tm % SUBLANES == 0
    hb = tm // SUBLANES
    kern = functools.partial(_conv_prompt_kernel, seq=seq, tm=tm)
    return pl.pallas_call(
        kern,
        grid=(n_rows // tm, c // tc),
        in_specs=[pl.BlockSpec((tm, tc), lambda i, j: (i, j)),
                  pl.BlockSpec((SUBLANES, tc), lambda i, j: (jnp.maximum(i * hb - 1, 0), j)),
                  pl.BlockSpec((tm, tc), lambda i, j: (i, j)),
                  pl.BlockSpec((CONV_K, tc), lambda i, j: (0, j))],
        out_specs=pl.BlockSpec((tm, tc), lambda i, j: (i, j)),
        out_shape=jax.ShapeDtypeStruct((n_rows, c), _BF16),
        compiler_params=_params(("arbitrary", "arbitrary")),
        name="conv_prompt",
    )(u, u, gate_b, conv_w)


def _conv_sample_kernel(u_ref, p0_ref, p1_ref, gb_ref, w_ref, o_ref):
    w = w_ref[...]
    y = w[0:1, :] * p0_ref[...] + w[1:2, :] * p1_ref[...] + w[2:3, :] * u_ref[...]
    o_ref[...] = (gb_ref[...].astype(_F32) * y).astype(o_ref.dtype)


def _conv_sample(u_s, prev0, prev1, gate_b_s, conv_w, tc):
    db, c = u_s.shape
    spec = pl.BlockSpec((db, tc), lambda j: (0, j))
    return pl.pallas_call(
        _conv_sample_kernel,
        grid=(c // tc,),
        in_specs=[spec, spec, spec, spec, pl.BlockSpec((CONV_K, tc), lambda j: (0, j))],
        out_specs=spec,
        out_shape=jax.ShapeDtypeStruct((db, c), _BF16),
        compiler_params=_params(("arbitrary",)),
        name="conv_sample",
    )(u_s, prev0, prev1, gate_b_s, conv_w)


def _topk_desc(scores, vals_refs, rank_refs):
    n_keys = scores[0].shape[0]
    key_id = lax.broadcasted_iota(jnp.int32, scores[0].shape, 0).astype(_F32)
    for rank_ref in rank_refs:
        rank_ref[...] = jnp.full(scores[0].shape, float(PEER_TOPK), _F32)

    def body(a, carry):
        out = []
        for s, vals_ref, rank_ref in zip(carry, vals_refs, rank_refs):
            m = jnp.max(s, axis=0, keepdims=True)
            first = jnp.min(jnp.where(s == m, key_id, float(n_keys)), axis=0, keepdims=True)
            sel = key_id == first
            vals_ref[pl.ds(a, 1), :] = m
            rank_ref[...] = jnp.where(sel, a.astype(_F32), rank_ref[...])
            out.append(jnp.where(sel, NEG_INF, s))
        return tuple(out)

    lax.fori_loop(0, PEER_TOPK, body, tuple(scores))


def _routing_kernel(qpT_ref, sk_ref, nb1_ref, a1_ref, r2_ref, a2_ref, v1_scr, v2_scr, r1_scr, *,
                    p_heads):
    k = PEER_TOPK
    tm = qpT_ref.shape[1]
    row8 = lax.broadcasted_iota(jnp.int32, (SUBLANES, tm), 0)
    row16 = lax.broadcasted_iota(jnp.int32, (2 * SUBLANES, tm), 0)

    pos_parts = [(row16 * k).astype(_F32)]
    valid_parts = [row16 >= 0]
    for b in range(1, SUBLANES):
        pos_parts.append((row8 * k + b).astype(_F32))
        valid_parts.append((row8 + 1) * (b + 1) <= k)
    pos_parts.append((row8 + SUBLANES).astype(_F32))
    valid_parts.append(row8 >= 0)
    pos = jnp.concatenate(pos_parts, axis=0)
    valid = jnp.concatenate(valid_parts, axis=0)

    def per_head(h, carry):
        q1 = qpT_ref[pl.ds(pl.multiple_of(h * 2 * PEER_HALF, PEER_HALF), PEER_HALF), :]
        q2 = qpT_ref[pl.ds(pl.multiple_of(h * 2 * PEER_HALF + PEER_HALF, PEER_HALF), PEER_HALF), :]
        s1 = _dot(sk_ref[h, 0].astype(_BF16), q1.astype(_BF16))
        s2 = _dot(sk_ref[h, 1].astype(_BF16), q2.astype(_BF16))
        _topk_desc([s1, s2], [v1_scr, v2_scr], [r1_scr, r2_ref.at[h]])
        v1 = v1_scr[...]
        v2 = v2_scr[...]

        parts = [v1 + v2[0:1, :]]
        for b in range(1, SUBLANES):
            parts.append(v1[0:SUBLANES, :] + v2[b:b + 1, :])
        parts.append(v1[0:1, :] + v2[SUBLANES:2 * SUBLANES, :])
        cand = jnp.where(valid, jnp.concatenate(parts, axis=0), NEG_INF)
        cmax = v1[0:1, :] + v2[0:1, :]

        def pick(_, carry):
            cand, chosen, z = carry
            m = jnp.max(cand, axis=0, keepdims=True)
            first = jnp.min(jnp.where(cand == m, pos, float(k * k)), axis=0, keepdims=True)
            sel = pos == first
            chosen = jnp.where(sel, 1.0, chosen)
            cand = jnp.where(sel, NEG_INF, cand)
            return cand, chosen, z + jnp.exp(m - cmax)

        _, chosen, z = lax.fori_loop(
            0, k, pick, (cand, jnp.zeros(cand.shape, _F32), jnp.zeros((1, tm), _F32)))

        nb_lo = chosen[0:SUBLANES, :]
        for b in range(1, SUBLANES):
            nb_lo = nb_lo + chosen[2 * SUBLANES + (b - 1) * SUBLANES:2 * SUBLANES + b * SUBLANES, :]
        tail = jnp.sum(chosen[2 * SUBLANES + 7 * SUBLANES:, :], axis=0, keepdims=True)
        nb_lo = nb_lo + jnp.where(row8 == 0, tail, 0.0)
        nb = jnp.concatenate([nb_lo, chosen[SUBLANES:2 * SUBLANES, :]], axis=0)

        r1 = r1_scr[...]
        nb1 = jnp.zeros(r1.shape, _F32)
        for a in range(k):
            nb1 = jnp.where(r1 == float(a), nb[a:a + 1, :], nb1)
        nb1_ref[h] = nb1
        a1_ref[h] = jnp.exp(s1 - v1[0:1, :])
        a2_ref[h] = jnp.exp(s2 - v2[0:1, :]) / z
        return carry

    lax.fori_loop(0, p_heads, per_head, 0)


def _routing(qpT, subkeys, tmr):
    p_heads = subkeys.shape[0]
    n = qpT.shape[1]
    assert subkeys.shape[1:] == (2, PEER_N_KEYS, PEER_HALF)
    out = jax.ShapeDtypeStruct((p_heads, PEER_N_KEYS, n), _F32)
    spec = pl.BlockSpec((p_heads, PEER_N_KEYS, tmr), lambda i: (0, 0, i))
    return pl.pallas_call(
        functools.partial(_routing_kernel, p_heads=p_heads),
        grid=(n // tmr,),
        in_specs=[pl.BlockSpec((qpT.shape[0], tmr), lambda i: (0, i)),
                  pl.BlockSpec(subkeys.shape, lambda i: (0, 0, 0, 0))],
        out_specs=[spec] * 4,
        out_shape=[out] * 4,
        scratch_shapes=[pltpu.VMEM((PEER_TOPK, tmr), _F32)] * 2 + [pltpu.VMEM((PEER_N_KEYS, tmr), _F32)],
        compiler_params=_params(("arbitrary",)),
        name="peer_routing",
    )(qpT, subkeys)


def _peer_kernel(hT_ref, u_ref, v_ref, nb1_ref, a1_ref, r2_ref, a2_ref, o_ref, acc_scr, ga_scr, *,
                 p_heads, te, n_tiles):
    j = pl.program_id(1)
    n_i1 = te // PEER_N_KEYS

    def weigh(slot):
        act = _dot(u_ref[...], hT_ref[...])
        act = 0.5 * act * (1.0 + lax.erf(act * (0.5 ** 0.5)))
        for c in range(n_i1):
            g = jnp.zeros((PEER_N_KEYS, act.shape[1]), _F32)
            for h in range(p_heads):
                nb = nb1_ref[c, h:h + 1, :]
                a1 = a1_ref[c, h:h + 1, :]
                g = g + jnp.where(r2_ref[h] < nb, a2_ref[h] * a1, 0.0)
            rows = slice(c * PEER_N_KEYS, (c + 1) * PEER_N_KEYS)
            ga_scr[slot, rows, :] = (g * act[rows, :]).astype(_BF16)

    def accumulate(slot):
        acc_scr[...] += _dot(ga_scr[slot], v_ref[...], (0, 0))

    @pl.when(j == 0)
    def _():
        acc_scr[...] = jnp.zeros_like(acc_scr)
        weigh(0)

    for parity in range(2):
        @pl.when((j > 0) & (j < n_tiles) & (j % 2 == parity))
        def _(parity=parity):
            accumulate(1 - parity)
            weigh(parity)

    @pl.when(j == n_tiles)
    def _():
        accumulate((n_tiles - 1) % 2)
        o_ref[...] = acc_scr[...].astype(o_ref.dtype)


def _peer(hT, u_bf, v_bf, routing, tm, te):
    d, n = hT.shape
    n_exp = u_bf.shape[0]
    nb1, a1, r2, a2 = routing
    p_heads = r2.shape[0]
    n_i1 = te // PEER_N_KEYS
    nb1 = jnp.transpose(nb1, (1, 0, 2))
    a1 = jnp.transpose(a1, (1, 0, 2))
    n_tiles = n_exp // te
    last = n_tiles - 1
    k1spec = pl.BlockSpec((n_i1, p_heads, tm), lambda i, j: (jnp.minimum(j, last), 0, i))
    k2spec = pl.BlockSpec((p_heads, PEER_N_KEYS, tm), lambda i, j: (0, 0, i),
                          pipeline_mode=pl.Buffered(1))
    kern = functools.partial(_peer_kernel, p_heads=p_heads, te=te, n_tiles=n_tiles)
    return pl.pallas_call(
        kern,
        grid=(n // tm, n_tiles + 1),
        in_specs=[pl.BlockSpec((d, tm), lambda i, j: (0, i), pipeline_mode=pl.Buffered(1)),
                  pl.BlockSpec((te, d), lambda i, j: (jnp.minimum(j, last), 0)),
                  pl.BlockSpec((te, d), lambda i, j: (jnp.maximum(j - 1, 0), 0)),
                  k1spec, k1spec, k2spec, k2spec],
        out_specs=pl.BlockSpec((tm, d), lambda i, j: (i, 0)),
        out_shape=jax.ShapeDtypeStruct((n, d), _BF16),
        scratch_shapes=[pltpu.VMEM((tm, d), _F32), pltpu.VMEM((2, te, tm), _BF16)],
        compiler_params=_params(("arbitrary", "arbitrary")),
        name="peer_experts",
    )(hT, u_bf, v_bf, nb1, a1, r2, a2)


def _layer(x_prompt, x_sample, caches, layer, state_conv, norm1_gain, w_in, q_norm_gain, k_norm_gain,
           conv_w, w_attn_out, w_conv_out, w_out, norm2_gain, w_peer_q, peer_subkeys, peer_u, peer_v):
    batch, seq, d = x_prompt.shape
    db, dec_seq, _ = x_sample.shape
    assert dec_seq == 1
    width = w_attn_out.shape[0]
    heads = width // HEAD_DIM
    attn_w = N_GROUPS * width
    c = conv_w.shape[1]
    n_p = batch * seq
    n = n_p + db
    assert w_in.shape[1] == 3 * attn_w + 3 * c + 2 * d

    tm = _pick_tile(n, 640, LANES)
    tn = 512
    col_q, col_k, col_v = 0, attn_w, 2 * attn_w
    col_ci = 3 * attn_w
    col_gb, col_gc = col_ci + c, col_ci + 2 * c
    col_ga = col_ci + 3 * c

    x_all = jnp.concatenate([x_prompt.reshape(n_p, d), x_sample.reshape(db, d)], axis=0)
    xn = _rmsnorm(x_all, norm1_gain, tm)

    qk_gain = jnp.concatenate([jnp.tile(q_norm_gain, N_GROUPS * heads),
                               jnp.tile(k_norm_gain, N_GROUPS * heads)]).reshape(1, 2 * attn_w)
    (qk,) = _matmul([xn], [(w_in, col_q)], [(0, 0)], _ep_headnorm, [_F32], 2 * attn_w, tm, tn,
                    extras=[(qk_gain, "row", 0)], name="proj_qk")
    (v,) = _matmul([xn], [(w_in, col_v)], [(0, 0)], _ep_identity, [_F32], attn_w, tm, tn,
                   name="proj_v")
    (u,) = _matmul([xn], [(w_in, col_ci), (w_in, col_gc)], [(0, 0), (0, 1)], _ep_product,
                   [_F32], c, tm, 256, name="proj_u")
    (gate_b,) = _matmul([xn], [(w_in, col_gb)], [(0, 0)], _ep_identity, [_BF16], c, tm, tn,
                        name="proj_gate_b")
    (sig,) = _matmul([xn], [(w_in, col_ga)], [(0, 0)], _ep_sigmoid, [_BF16], 2 * d, tm, tn,
                     name="proj_sigmoid_gates")

    attn_p = _attn_prompt(qk, v, batch, seq, heads)
    attn_s = _attn_sample(qk[n_p:], v[n_p:], caches, layer, heads)
    attn = jnp.concatenate([attn_p, attn_s.astype(_BF16)], axis=0)

    tcv = _pick_tile(c, 1024, LANES)
    cb_p = _conv_prompt(u, gate_b, conv_w, n_p, seq, _pick_tile(seq, 512, SUBLANES), tcv)
    u_s = u[n_p:]
    cb_s = _conv_sample(u_s, state_conv[:, 0], state_conv[:, 1], gate_b[n_p:], conv_w, tcv)
    cb = jnp.concatenate([cb_p, cb_s], axis=0)

    (merged,) = _matmul([attn, cb], [(w_attn_out, 0), (w_conv_out, 0)], [(0, 0), (1, 1)],
                        _ep_merge, [_BF16], d, tm, tn,
                        extras=[(sig, "tile", 0), (sig, "tile", d)], name="merge")
    (x1,) = _matmul([merged], [(w_out, 0)], [(0, 0)], _ep_residual, [_F32], d, tm, tn,
                    extras=[(x_all, "tile", 0)], name="out_proj")

    hn, hnT = _rmsnorm(x1, norm2_gain, tm, transpose=True)
    (qpT,) = _matmul([hn], [(w_peer_q, 0)], [(0, 0)], _ep_identity, [_F32], w_peer_q.shape[1],
                     tm, tn, transpose_out=True, name="peer_query")
    routing = _routing(qpT, peer_subkeys, _pick_tile(n, 2 * LANES, LANES))
    te = 4 * PEER_N_KEYS
    peer_out = _peer(hnT, peer_u.astype(_BF16), peer_v.astype(_BF16), routing, tm, te)

    y_p = (x1[:n_p] + peer_out[:n_p].astype(_F32)).reshape(batch, seq, d)
    y_s = (x1[n_p:] + peer_out[n_p:].astype(_F32)).reshape(db, 1, d)
    kv_p, kv_s = [], []
    for g, (window, _) in enumerate(DILATED_GROUPS):
        kg = qk[:, attn_w + g * width:attn_w + (g + 1) * width]
        vg = v[:, g * width:(g + 1) * width]
        keep = min(window, seq)
        kp = kg[:n_p].reshape(batch, seq, heads, HEAD_DIM)[:, seq - keep:]
        vp = vg[:n_p].reshape(batch, seq, heads, HEAD_DIM)[:, seq - keep:]
        kv_p.append(jnp.stack([kp, vp], axis=2))
        ks = kg[n_p:].reshape(db, 1, heads, HEAD_DIM)
        vs = vg[n_p:].reshape(db, 1, heads, HEAD_DIM)
        kv_s.append(jnp.stack([ks, vs], axis=2))
    conv_p = u[:n_p].reshape(batch, seq, c)[:, seq - (CONV_K - 1):]
    conv_s = jnp.stack([state_conv[:, 1], u_s], axis=1)
    return y_p, y_s, kv_p, conv_p, kv_s, conv_s


def kernel(x_prompt, x_sample, cache_kv_w128, cache_kv_w512, cache_kv_w2048, state_conv, norm1_gain,
           w_in, q_norm_gain, k_norm_gain, conv_w, w_attn_out, w_conv_out, w_out, norm2_gain,
           w_peer_q, peer_subkeys, peer_u, peer_v):
    depth = w_in.shape[0]
    yp, ys = x_prompt, x_sample
    kv_p = [[] for _ in range(N_GROUPS)]
    kv_s = [[] for _ in range(N_GROUPS)]
    conv_p, conv_s = [], []
    for layer in range(depth):
        caches = (cache_kv_w128, cache_kv_w512, cache_kv_w2048)
        yp, ys, kvp, cp, kvs, cs = _layer(
            yp, ys, caches, layer, state_conv[layer], norm1_gain[layer], w_in[layer], q_norm_gain[layer],
            k_norm_gain[layer], conv_w[layer], w_attn_out[layer], w_conv_out[layer], w_out[layer],
            norm2_gain[layer], w_peer_q[layer], peer_subkeys[layer], peer_u[layer], peer_v[layer])
        for g in range(N_GROUPS):
            kv_p[g].append(kvp[g])
            kv_s[g].append(kvs[g])
        conv_p.append(cp)
        conv_s.append(cs)
    return (yp, ys,
            jnp.stack(kv_p[0]), jnp.stack(kv_p[1]), jnp.stack(kv_p[2]), jnp.stack(conv_p),
            jnp.stack(kv_s[0]), jnp.stack(kv_s[1]), jnp.stack(kv_s[2]), jnp.stack(conv_s))
```

```python
import functools
import math

import jax
import jax.numpy as jnp
from jax import lax
from jax.experimental import pallas as pl
from jax.experimental.pallas import tpu as pltpu

HEAD_DIM = 128
DILATED_GROUPS = ((128, 1), (512, 4), (2048, 16))
N_GROUPS = len(DILATED_GROUPS)
CONV_K = 3
PEER_N_KEYS = 128
PEER_HALF = 128
PEER_TOPK = 16
EPS = 1e-6

LANES = 128
SUBLANES = 8
VMEM_LIMIT_BYTES = 56 * 1024 * 1024
NEG_INF = float("-inf")

_BF16 = jnp.bfloat16
_F32 = jnp.float32


def _params(sem):
    return pltpu.CompilerParams(dimension_semantics=sem, vmem_limit_bytes=VMEM_LIMIT_BYTES)


def _dot(a, b, contract=(1, 0)):
    return lax.dot_general(a, b, (((contract[0],), (contract[1],)), ((), ())),
                           preferred_element_type=_F32)


def _pick_tile(n, cap, mult):
    best = None
    t = mult
    while t <= min(n, cap):
        if n % t == 0:
            best = t
        t += mult
    assert best is not None, (n, cap, mult)
    return best


def _norm_kernel(x_ref, g_ref, o_ref, *rest, transpose):
    x = x_ref[...]
    ms = jnp.mean(x * x, axis=-1, keepdims=True)
    y = x * lax.rsqrt(ms + EPS) * g_ref[...]
    o_ref[...] = y.astype(o_ref.dtype)
    if transpose:
        rest[0][...] = y.T.astype(rest[0].dtype)


def _rmsnorm(x, gain, tm, transpose=False):
    n, d = x.shape
    out_shape = [jax.ShapeDtypeStruct((n, d), _BF16)]
    out_specs = [pl.BlockSpec((tm, d), lambda i: (i, 0))]
    if transpose:
        out_shape.append(jax.ShapeDtypeStruct((d, n), _BF16))
        out_specs.append(pl.BlockSpec((d, tm), lambda i: (0, i)))
    res = pl.pallas_call(
        functools.partial(_norm_kernel, transpose=transpose),
        grid=(n // tm,),
        in_specs=[pl.BlockSpec((tm, d), lambda i: (i, 0)),
                  pl.BlockSpec((1, d), lambda i: (0, 0))],
        out_specs=out_specs,
        out_shape=out_shape,
        compiler_params=_params(("arbitrary",)),
        name="rmsnorm_t" if transpose else "rmsnorm",
    )(x, gain.reshape(1, d))
    return res if transpose else res[0]


def _mm_kernel(*refs, n_lhs, n_w, pairs, n_extra, n_out, epilogue, transpose_out):
    lhs = refs[:n_lhs]
    ws = refs[n_lhs:n_lhs + n_w]
    extra = refs[n_lhs + n_w:n_lhs + n_w + n_extra]
    outs = refs[n_lhs + n_w + n_extra:n_lhs + n_w + n_extra + n_out]

    w_bf = [w_ref[...].astype(_BF16) for w_ref in ws]
    accs = [_dot(lhs[a][...], w_bf[b]) for a, b in pairs]
    res = epilogue(accs, [e[...] for e in extra])
    for o_ref, r in zip(outs, res):
        if transpose_out:
            r = r.T
        o_ref[...] = r.astype(o_ref.dtype)


def _matmul(lhs_list, w_list, pairs, epilogue, out_dtypes, n_cols, tm, tn,
            extras=(), transpose_out=False, name="matmul"):
    n = lhs_list[0].shape[0]
    grid = (n // tm, n_cols // tn)
    in_specs, args = [], []
    for a in lhs_list:
        in_specs.append(pl.BlockSpec((tm, a.shape[1]), lambda i, j: (i, 0),
                                     pipeline_mode=pl.Buffered(1)))
        args.append(a)
    for w, off in w_list:
        assert off % tn == 0
        in_specs.append(pl.BlockSpec((w.shape[0], tn), lambda i, j, o=off // tn: (0, o + j)))
        args.append(w)
    for arr, kind, off in extras:
        assert off % tn == 0
        if kind == "tile":
            in_specs.append(pl.BlockSpec((tm, tn), lambda i, j, o=off // tn: (i, o + j)))
        else:
            in_specs.append(pl.BlockSpec((1, tn), lambda i, j, o=off // tn: (0, o + j)))
        args.append(arr)
    if transpose_out:
        out_shape = [jax.ShapeDtypeStruct((n_cols, n), dt) for dt in out_dtypes]
        out_specs = [pl.BlockSpec((tn, tm), lambda i, j: (j, i)) for _ in out_dtypes]
    else:
        out_shape = [jax.ShapeDtypeStruct((n, n_cols), dt) for dt in out_dtypes]
        out_specs = [pl.BlockSpec((tm, tn), lambda i, j: (i, j)) for _ in out_dtypes]
    kern = functools.partial(
        _mm_kernel, n_lhs=len(lhs_list), n_w=len(w_list), pairs=tuple(pairs),
        n_extra=len(extras), n_out=len(out_dtypes), epilogue=epilogue,
        transpose_out=transpose_out)
    return pl.pallas_call(
        kern, grid=grid, in_specs=in_specs, out_specs=out_specs, out_shape=out_shape,
        compiler_params=_params(("arbitrary", "arbitrary")),
        name=name,
    )(*args)


def _ep_identity(accs, extras):
    return [accs[0]]


def _ep_headnorm(accs, extras):
    acc, gain = accs[0], extras[0]
    cols = []
    for c in range(acc.shape[1] // HEAD_DIM):
        blk = acc[:, c * HEAD_DIM:(c + 1) * HEAD_DIM]
        ms = jnp.mean(blk * blk, axis=-1, keepdims=True)
        cols.append(blk * lax.rsqrt(ms + EPS) * gain[:, c * HEAD_DIM:(c + 1) * HEAD_DIM])
    return [jnp.concatenate(cols, axis=1) if len(cols) > 1 else cols[0]]


def _ep_product(accs, extras):
    return [accs[0] * accs[1]]


def _ep_sigmoid(accs, extras):
    return [jax.nn.sigmoid(accs[0])]


def _ep_merge(accs, extras):
    return [extras[0].astype(_F32) * accs[0] + extras[1].astype(_F32) * accs[1]]


def _ep_residual(accs, extras):
    return [extras[0] + accs[0]]


def _attn_prompt_kernel(q_ref, k_ref, v_ref, o_ref, m_scr, l_scr, acc_scr, *, seq, groups):
    step = pl.program_id(2)
    scale = HEAD_DIM ** -0.5
    bq = LANES
    n_steps = len(groups)

    def run_group(window, dil, first, last):
        band = window // dil
        sub_len = seq // dil
        nblk = sub_len // bq
        nkeys = band + bq
        assert sub_len >= nkeys and band == bq

        def body(idx, carry):
            r = idx // nblk
            blk = idx % nblk
            s0 = blk * bq
            k0 = jnp.maximum(s0 - band, 0)
            if dil == 1:
                q_rows = pl.ds(pl.multiple_of(s0, bq), bq)
                k_rows = pl.ds(pl.multiple_of(k0, bq), nkeys)
            else:
                q_rows = pl.ds(r + dil * s0, bq, stride=dil)
                k_rows = pl.ds(r + dil * k0, nkeys, stride=dil)
            q = q_ref[q_rows, :].astype(_BF16)
            k = k_ref[k_rows, :].astype(_BF16)
            v = v_ref[k_rows, :].astype(_BF16)
            s = _dot(q, k, (1, 1)) * scale
            qi = s0 + lax.broadcasted_iota(jnp.int32, (bq, nkeys), 0)
            ki = k0 + lax.broadcasted_iota(jnp.int32, (bq, nkeys), 1)
            diff = qi - ki
            s = jnp.where((diff >= 0) & (diff <= band), s, NEG_INF)
            m_b = jnp.max(s, axis=-1, keepdims=True)
            p = jnp.exp(s - m_b)
            l_b = jnp.sum(p, axis=-1, keepdims=True)
            acc_b = _dot(p.astype(_BF16), v)
            m_b = jnp.broadcast_to(m_b, (bq, LANES))
            l_b = jnp.broadcast_to(l_b, (bq, LANES))
            if not first:
                m_old = m_scr[q_rows, :]
                m_new = jnp.maximum(m_old, m_b)
                a_old = jnp.exp(m_old - m_new)
                a_b = jnp.exp(m_b - m_new)
                l_b = a_old * l_scr[q_rows, :] + a_b * l_b
                acc_b = a_old * acc_scr[q_rows, :] + a_b * acc_b
                m_b = m_new
            if last:
                o_ref[q_rows, :] = (acc_b / l_b).astype(o_ref.dtype)
            else:
                m_scr[q_rows, :] = m_b
                l_scr[q_rows, :] = l_b
                acc_scr[q_rows, :] = acc_b
            return carry

        lax.fori_loop(0, dil * nblk, body, 0, unroll=8)

    order = sorted(range(n_steps), key=lambda g: -groups[g][1])
    assert groups[order[-1]][1] == 1
    for pos, g in enumerate(order):
        @pl.when(step == pos)
        def _(g=g, pos=pos):
            run_group(groups[g][0], groups[g][1], pos == 0, pos == n_steps - 1)


def _attn_prompt(qk, v, batch, seq, heads):
    n_steps = N_GROUPS
    order = sorted(range(N_GROUPS), key=lambda g: -DILATED_GROUPS[g][1])
    gh = N_GROUPS * heads

    def col(step, h):
        g = jnp.int32(order[0])
        for pos in range(1, n_steps):
            g = jnp.where(step == pos, order[pos], g)
        return g * heads + h

    kern = functools.partial(_attn_prompt_kernel, seq=seq, groups=DILATED_GROUPS)
    return pl.pallas_call(
        kern,
        grid=(batch, heads, n_steps),
        in_specs=[
            pl.BlockSpec((seq, HEAD_DIM), lambda b, h, s: (b, col(s, h))),
            pl.BlockSpec((seq, HEAD_DIM), lambda b, h, s: (b, gh + col(s, h))),
            pl.BlockSpec((seq, HEAD_DIM), lambda b, h, s: (b, col(s, h))),
        ],
        out_specs=pl.BlockSpec((seq, HEAD_DIM), lambda b, h, s: (b, h)),
        out_shape=jax.ShapeDtypeStruct((batch * seq, heads * HEAD_DIM), _BF16),
        scratch_shapes=[pltpu.VMEM((seq, LANES), _F32)] * 3,
        compiler_params=_params(("arbitrary", "arbitrary", "arbitrary")),
        name="attn_prompt",
    )(qk, qk, v)


def _attn_sample_kernel(q_ref, v_ref, c0_ref, c1_ref, c2_ref, o_ref, *, heads, bb):
    scale = HEAD_DIM ** -0.5
    caches = (c0_ref, c1_ref, c2_ref)
    for bi in range(bb):
        s_old, s_new = [], []
        for g in range(N_GROUPS):
            q = q_ref[bi, g * heads:(g + 1) * heads, :]
            kn = q_ref[bi, (N_GROUPS + g) * heads:(N_GROUPS + g + 1) * heads, :]
            s_old.append(jnp.sum(caches[g][bi, :, 0] * q[None], axis=-1, keepdims=True) * scale)
            s_new.append(jnp.sum(kn * q, axis=-1, keepdims=True) * scale)
        m = s_new[0]
        for g in range(N_GROUPS):
            m = jnp.maximum(m, jnp.maximum(s_new[g], jnp.max(s_old[g], axis=0)))
        l = jnp.zeros((heads, 1), _F32)
        acc = jnp.zeros((heads, HEAD_DIM), _F32)
        for g in range(N_GROUPS):
            p_old = jnp.exp(s_old[g] - m[None])
            p_new = jnp.exp(s_new[g] - m)
            vn = v_ref[bi, g * heads:(g + 1) * heads, :]
            l = l + jnp.sum(p_old, axis=0) + p_new
            acc = acc + jnp.sum(p_old * caches[g][bi, :, 1], axis=0) + p_new * vn
        o_ref[bi] = (acc / l).astype(o_ref.dtype)


def _attn_sample(qk_s, v_s, caches, layer, heads):
    db = qk_s.shape[0]
    width = heads * HEAD_DIM
    bb = 4
    assert db % bb == 0
    cache_args, cache_specs = [], []
    for (window, dil), c in zip(DILATED_GROUPS, caches):
        depth, _, past = c.shape[:3]
        assert past == window
        taps = past // dil
        cache_args.append(c.reshape(depth, db, taps, dil, 2, heads, HEAD_DIM))
        cache_specs.append(pl.BlockSpec((None, bb, taps, None, 2, heads, HEAD_DIM),
                                        lambda i, layer=layer: (layer, i, 0, 0, 0, 0, 0)))
    kern = functools.partial(_attn_sample_kernel, heads=heads, bb=bb)
    out = pl.pallas_call(
        kern,
        grid=(db // bb,),
        in_specs=[pl.BlockSpec((bb, 2 * N_GROUPS * heads, HEAD_DIM), lambda i: (i, 0, 0)),
                  pl.BlockSpec((bb, N_GROUPS * heads, HEAD_DIM), lambda i: (i, 0, 0))] + cache_specs,
        out_specs=pl.BlockSpec((bb, heads, HEAD_DIM), lambda i: (i, 0, 0)),
        out_shape=jax.ShapeDtypeStruct((db, heads, HEAD_DIM), _F32),
        compiler_params=_params(("arbitrary",)),
        name="attn_sample",
    )(qk_s.reshape(db, -1, HEAD_DIM), v_s.reshape(db, -1, HEAD_DIM), *cache_args)
    return out.reshape(db, width)


def _conv_prompt_kernel(u_ref, halo_ref, gb_ref, w_ref, o_ref, *, seq, tm):
    i = pl.program_id(0)
    u = u_ref[...]
    halo = halo_ref[...]
    halo = jnp.where((i * tm) % seq == 0, 0.0, halo)
    h6 = halo[SUBLANES - 2:SUBLANES - 1, :]
    h7 = halo[SUBLANES - 1:SUBLANES, :]
    rows = lax.broadcasted_iota(jnp.int32, u.shape, 0)
    um1 = jnp.where(rows == 0, h7, pltpu.roll(u, 1, 0))
    um2 = jnp.where(rows == 0, h6, jnp.where(rows == 1, h7, pltpu.roll(u, 2, 0)))
    w = w_ref[...]
    y = w[0:1, :] * um2 + w[1:2, :] * um1 + w[2:3, :] * u
    o_ref[...] = (gb_ref[...].astype(_F32) * y).astype(o_ref.dtype)


def _conv_prompt(u, gate_b, conv_w, n_rows, seq, tm, tc):
    c = u.shape[1]
    assert seq % tm == 0 and tm % SUBLANES == 0
    hb = tm // SUBLANES
    kern = functools.partial(_conv_prompt_kernel, seq=seq, tm=tm)
    return pl.pallas_call(
        kern,
        grid=(n_rows // tm, c // tc),
        in_specs=[pl.BlockSpec((tm, tc), lambda i, j: (i, j)),
                  pl.BlockSpec((SUBLANES, tc), lambda i, j: (jnp.maximum(i * hb - 1, 0), j)),
                  pl.BlockSpec((tm, tc), lambda i, j: (i, j)),
                  pl.BlockSpec((CONV_K, tc), lambda i, j: (0, j))],
        out_specs=pl.BlockSpec((tm, tc), lambda i, j: (i, j)),
        out_shape=jax.ShapeDtypeStruct((n_rows, c), _BF16),
        compiler_params=_params(("arbitrary", "arbitrary")),
        name="conv_prompt",
    )(u, u, gate_b, conv_w)


def _conv_sample_kernel(u_ref, p0_ref, p1_ref, gb_ref, w_ref, o_ref):
    w = w_ref[...]
    y = w[0:1, :] * p0_ref[...] + w[1:2, :] * p1_ref[...] + w[2:3, :] * u_ref[...]
    o_ref[...] = (gb_ref[...].astype(_F32) * y).astype(o_ref.dtype)


def _conv_sample(u_s, prev0, prev1, gate_b_s, conv_w, tc):
    db, c = u_s.shape
    spec = pl.BlockSpec((db, tc), lambda j: (0, j))
    return pl.pallas_call(
        _conv_sample_kernel,
        grid=(c // tc,),
        in_specs=[spec, spec, spec, spec, pl.BlockSpec((CONV_K, tc), lambda j: (0, j))],
        out_specs=spec,
        out_shape=jax.ShapeDtypeStruct((db, c), _BF16),
        compiler_params=_params(("arbitrary",)),
        name="conv_sample",
    )(u_s, prev0, prev1, gate_b_s, conv_w)


def _topk_desc(scores, vals_refs, rank_refs):
    n_keys = scores[0].shape[0]
    key_id = lax.broadcasted_iota(jnp.int32, scores[0].shape, 0).astype(_F32)

    def run(break_ties):
        for rank_ref in rank_refs:
            rank_ref[...] = jnp.full(scores[0].shape, float(PEER_TOPK), _F32)

        def body(a, carry):
            out = []
            for s, vals_ref, rank_ref in zip(carry, vals_refs, rank_refs):
                m = jnp.max(s, axis=0, keepdims=True)
                sel = s == m
                if break_ties:
                    first = jnp.min(jnp.where(sel, key_id, float(n_keys)), axis=0, keepdims=True)
                    sel = key_id == first
                vals_ref[pl.ds(a, 1), :] = m
                rank_ref[...] = jnp.where(sel, lax.convert_element_type(a, _F32), rank_ref[...])
                out.append(jnp.where(sel, NEG_INF, s))
            return tuple(out)

        lax.fori_loop(0, PEER_TOPK, body, tuple(scores))

    run(break_ties=False)
    n_off = jnp.zeros((), _F32)
    for rank_ref in rank_refs:
        taken = jnp.sum(jnp.where(rank_ref[...] < float(PEER_TOPK), 1.0, 0.0), axis=0, keepdims=True)
        n_off = n_off + jnp.sum(jnp.where(taken == float(PEER_TOPK), 0.0, 1.0))

    @pl.when(n_off > 0.0)
    def _():
        run(break_ties=True)


def _routing_kernel(qpT_ref, sk_ref, nb1_ref, a1_ref, r2_ref, a2_ref, v1_scr, v2_scr, r1_scr, *,
                    p_heads):
    k = PEER_TOPK
    tm = qpT_ref.shape[1]
    row8 = lax.broadcasted_iota(jnp.int32, (SUBLANES, tm), 0)
    row16 = lax.broadcasted_iota(jnp.int32, (2 * SUBLANES, tm), 0)

    pos_parts = [(row16 * k).astype(_F32)]
    valid_parts = [row16 >= 0]
    for b in range(1, SUBLANES):
        pos_parts.append((row8 * k + b).astype(_F32))
        valid_parts.append((row8 + 1) * (b + 1) <= k)
    pos_parts.append((row8 + SUBLANES).astype(_F32))
    valid_parts.append(row8 >= 0)
    pos = jnp.concatenate(pos_parts, axis=0)
    valid = jnp.concatenate(valid_parts, axis=0)

    def per_head(h, carry):
        q1 = qpT_ref[pl.ds(pl.multiple_of(h * 2 * PEER_HALF, PEER_HALF), PEER_HALF), :]
        q2 = qpT_ref[pl.ds(pl.multiple_of(h * 2 * PEER_HALF + PEER_HALF, PEER_HALF), PEER_HALF), :]
        s1 = _dot(sk_ref[h, 0].astype(_BF16), q1.astype(_BF16))
        s2 = _dot(sk_ref[h, 1].astype(_BF16), q2.astype(_BF16))
        _topk_desc([s1, s2], [v1_scr, v2_scr], [r1_scr, r2_ref.at[h]])
        v1 = v1_scr[...]
        v2 = v2_scr[...]

        parts = [v1 + v2[0:1, :]]
        for b in range(1, SUBLANES):
            parts.append(v1[0:SUBLANES, :] + v2[b:b + 1, :])
        parts.append(v1[0:1, :] + v2[SUBLANES:2 * SUBLANES, :])
        cand = jnp.where(valid, jnp.concatenate(parts, axis=0), NEG_INF)
        cmax = v1[0:1, :] + v2[0:1, :]

        def pick_all(break_ties):
            def pick(_, carry):
                cand, chosen, z = carry
                m = jnp.max(cand, axis=0, keepdims=True)
                sel = cand == m
                if break_ties:
                    first = jnp.min(jnp.where(sel, pos, float(k * k)), axis=0, keepdims=True)
                    sel = pos == first
                chosen = jnp.where(sel, 1.0, chosen)
                cand = jnp.where(sel, NEG_INF, cand)
                return cand, chosen, z + jnp.exp(m - cmax)

            _, chosen, z = lax.fori_loop(
                0, k, pick, (cand, jnp.zeros(cand.shape, _F32), jnp.zeros((1, tm), _F32)))
            return chosen, z

        chosen, z = pick_all(break_ties=False)
        taken = jnp.sum(chosen, axis=0, keepdims=True)
        n_off = jnp.sum(jnp.where(taken == float(k), 0.0, 1.0))
        chosen, z = lax.cond(n_off > 0.0, lambda: pick_all(break_ties=True), lambda: (chosen, z))

        nb_lo = chosen[0:SUBLANES, :]
        for b in range(1, SUBLANES):
            nb_lo = nb_lo + chosen[2 * SUBLANES + (b - 1) * SUBLANES:2 * SUBLANES + b * SUBLANES, :]
        tail = jnp.sum(chosen[2 * SUBLANES + 7 * SUBLANES:, :], axis=0, keepdims=True)
        nb_lo = nb_lo + jnp.where(row8 == 0, tail, 0.0)
        nb = jnp.concatenate([nb_lo, chosen[SUBLANES:2 * SUBLANES, :]], axis=0)

        r1 = r1_scr[...]
        nb1 = jnp.zeros(r1.shape, _F32)
        for a in range(k):
            nb1 = jnp.where(r1 == float(a), nb[a:a + 1, :], nb1)
        nb1_ref[h] = nb1
        a1_ref[h] = jnp.exp(s1 - v1[0:1, :])
        a2_ref[h] = jnp.exp(s2 - v2[0:1, :]) / z
        return carry

    lax.fori_loop(0, p_heads, per_head, 0)


def _routing(qpT, subkeys, tmr):
    p_heads = subkeys.shape[0]
    n = qpT.shape[1]
    assert subkeys.shape[1:] == (2, PEER_N_KEYS, PEER_HALF)
    out = jax.ShapeDtypeStruct((p_heads, PEER_N_KEYS, n), _F32)
    spec = pl.BlockSpec((p_heads, PEER_N_KEYS, tmr), lambda i: (0, 0, i))
    return pl.pallas_call(
        functools.partial(_routing_kernel, p_heads=p_heads),
        grid=(n // tmr,),
        in_specs=[pl.BlockSpec((qpT.shape[0], tmr), lambda i: (0, i)),
                  pl.BlockSpec(subkeys.shape, lambda i: (0, 0, 0, 0))],
        out_specs=[spec] * 4,
        out_shape=[out] * 4,
        scratch_shapes=[pltpu.VMEM((PEER_TOPK, tmr), _F32)] * 2 + [pltpu.VMEM((PEER_N_KEYS, tmr), _F32)],
        compiler_params=_params(("arbitrary",)),
        name="peer_routing",
    )(qpT, subkeys)


def _peer_kernel(hT_ref, u_ref, v_ref, nb1_ref, a1_ref, r2_ref, a2_ref, o_ref, acc_scr, ga_scr, *,
                 p_heads, te, n_tiles):
    j = pl.program_id(1)
    n_i1 = te // PEER_N_KEYS

    def weigh(slot):
        act = _dot(u_ref[...], hT_ref[...])
        act = 0.5 * act * (1.0 + lax.erf(act * (0.5 ** 0.5)))
        for c in range(n_i1):
            g = jnp.zeros((PEER_N_KEYS, act.shape[1]), _F32)
            for h in range(p_heads):
                nb = nb1_ref[c, h:h + 1, :]
                a1 = a1_ref[c, h:h + 1, :]
                g = g + jnp.where(r2_ref[h] < nb, a2_ref[h] * a1, 0.0)
            rows = slice(c * PEER_N_KEYS, (c + 1) * PEER_N_KEYS)
            ga_scr[slot, rows, :] = (g * act[rows, :]).astype(_BF16)

    def accumulate(slot):
        acc_scr[...] += _dot(ga_scr[slot], v_ref[...], (0, 0))

    @pl.when(j == 0)
    def _():
        acc_scr[...] = jnp.zeros_like(acc_scr)
        weigh(0)

    for parity in range(2):
        @pl.when((j > 0) & (j < n_tiles) & (j % 2 == parity))
        def _(parity=parity):
            accumulate(1 - parity)
            weigh(parity)

    @pl.when(j == n_tiles)
    def _():
        accumulate((n_tiles - 1) % 2)
        o_ref[...] = acc_scr[...].astype(o_ref.dtype)


def _peer(hT, u_bf, v_bf, routing, tm, te):
    d, n = hT.shape
    n_exp = u_bf.shape[0]
    nb1, a1, r2, a2 = routing
    p_heads = r2.shape[0]
    n_i1 = te // PEER_N_KEYS
    nb1 = jnp.transpose(nb1, (1, 0, 2))
    a1 = jnp.transpose(a1, (1, 0, 2))
    n_tiles = n_exp // te
    last = n_tiles - 1
    k1spec = pl.BlockSpec((n_i1, p_heads, tm), lambda i, j: (jnp.minimum(j, last), 0, i))
    k2spec = pl.BlockSpec((p_heads, PEER_N_KEYS, tm), lambda i, j: (0, 0, i),
                          pipeline_mode=pl.Buffered(1))
    kern = functools.partial(_peer_kernel, p_heads=p_heads, te=te, n_tiles=n_tiles)
    return pl.pallas_call(
        kern,
        grid=(n // tm, n_tiles + 1),
        in_specs=[pl.BlockSpec((d, tm), lambda i, j: (0, i), pipeline_mode=pl.Buffered(1)),
                  pl.BlockSpec((te, d), lambda i, j: (jnp.minimum(j, last), 0)),
                  pl.BlockSpec((te, d), lambda i, j: (jnp.maximum(j - 1, 0), 0)),
                  k1spec, k1spec, k2spec, k2spec],
        out_specs=pl.BlockSpec((tm, d), lambda i, j: (i, 0)),
        out_shape=jax.ShapeDtypeStruct((n, d), _BF16),
        scratch_shapes=[pltpu.VMEM((tm, d), _F32), pltpu.VMEM((2, te, tm), _BF16)],
        compiler_params=_params(("arbitrary", "arbitrary")),
        name="peer_experts",
    )(hT, u_bf, v_bf, nb1, a1, r2, a2)


def _layer(x_prompt, x_sample, caches, layer, state_conv, norm1_gain, w_in, q_norm_gain, k_norm_gain,
           conv_w, w_attn_out, w_conv_out, w_out, norm2_gain, w_peer_q, peer_subkeys, peer_u, peer_v):
    batch, seq, d = x_prompt.shape
    db, dec_seq, _ = x_sample.shape
    assert dec_seq == 1
    width = w_attn_out.shape[0]
    heads = width // HEAD_DIM
    attn_w = N_GROUPS * width
    c = conv_w.shape[1]
    n_p = batch * seq
    n = n_p + db
    assert w_in.shape[1] == 3 * attn_w + 3 * c + 2 * d

    tm = _pick_tile(n, 640, LANES)
    tmm = _pick_tile(n, 1664, LANES)
    tn = 512
    col_q, col_k, col_v = 0, attn_w, 2 * attn_w
    col_ci = 3 * attn_w
    col_gb, col_gc = col_ci + c, col_ci + 2 * c
    col_ga = col_ci + 3 * c

    x_all = jnp.concatenate([x_prompt.reshape(n_p, d), x_sample.reshape(db, d)], axis=0)
    xn = _rmsnorm(x_all, norm1_gain, tm)

    qk_gain = jnp.concatenate([jnp.tile(q_norm_gain, N_GROUPS * heads),
                               jnp.tile(k_norm_gain, N_GROUPS * heads)]).reshape(1, 2 * attn_w)
    (qk,) = _matmul([xn], [(w_in, col_q)], [(0, 0)], _ep_headnorm, [_F32], 2 * attn_w, tmm, tn,
                    extras=[(qk_gain, "row", 0)], name="proj_qk")
    (v,) = _matmul([xn], [(w_in, col_v)], [(0, 0)], _ep_identity, [_F32], attn_w, tmm, tn,
                   name="proj_v")
    (u,) = _matmul([xn], [(w_in, col_ci), (w_in, col_gc)], [(0, 0), (0, 1)], _ep_product,
                   [_F32], c, tmm, 256, name="proj_u")
    (gate_b,) = _matmul([xn], [(w_in, col_gb)], [(0, 0)], _ep_identity, [_BF16], c, tmm, tn,
                        name="proj_gate_b")
    (sig,) = _matmul([xn], [(w_in, col_ga)], [(0, 0)], _ep_sigmoid, [_BF16], 2 * d, tmm, tn,
                     name="proj_sigmoid_gates")

    attn_p = _attn_prompt(qk, v, batch, seq, heads)
    attn_s = _attn_sample(qk[n_p:], v[n_p:], caches, layer, heads)
    attn = jnp.concatenate([attn_p, attn_s.astype(_BF16)], axis=0)

    tcv = _pick_tile(c, 1024, LANES)
    cb_p = _conv_prompt(u, gate_b, conv_w, n_p, seq, _pick_tile(seq, 512, SUBLANES), tcv)
    u_s = u[n_p:]
    cb_s = _conv_sample(u_s, state_conv[:, 0], state_conv[:, 1], gate_b[n_p:], conv_w, tcv)
    cb = jnp.concatenate([cb_p, cb_s], axis=0)

    (merged,) = _matmul([attn, cb], [(w_attn_out, 0), (w_conv_out, 0)], [(0, 0), (1, 1)],
                        _ep_merge, [_BF16], d, tmm, tn,
                        extras=[(sig, "tile", 0), (sig, "tile", d)], name="merge")
    (x1,) = _matmul([merged], [(w_out, 0)], [(0, 0)], _ep_residual, [_F32], d, tmm, tn,
                    extras=[(x_all, "tile", 0)], name="out_proj")

    hn, hnT = _rmsnorm(x1, norm2_gain, tm, transpose=True)
    (qpT,) = _matmul([hn], [(w_peer_q, 0)], [(0, 0)], _ep_identity, [_F32], w_peer_q.shape[1],
                     tmm, tn, transpose_out=True, name="peer_query")
    routing = _routing(qpT, peer_subkeys, _pick_tile(n, 2 * LANES, LANES))
    te = 4 * PEER_N_KEYS
    peer_out = _peer(hnT, peer_u.astype(_BF16), peer_v.astype(_BF16), routing, tm, te)

    y_p = (x1[:n_p] + peer_out[:n_p].astype(_F32)).reshape(batch, seq, d)
    y_s = (x1[n_p:] + peer_out[n_p:].astype(_F32)).reshape(db, 1, d)
    kv_p, kv_s = [], []
    for g, (window, _) in enumerate(DILATED_GROUPS):
        kg = qk[:, attn_w + g * width:attn_w + (g + 1) * width]
        vg = v[:, g * width:(g + 1) * width]
        keep = min(window, seq)
        kp = kg[:n_p].reshape(batch, seq, heads, HEAD_DIM)[:, seq - keep:]
        vp = vg[:n_p].reshape(batch, seq, heads, HEAD_DIM)[:, seq - keep:]
        kv_p.append(jnp.stack([kp, vp], axis=2))
        ks = kg[n_p:].reshape(db, 1, heads, HEAD_DIM)
        vs = vg[n_p:].reshape(db, 1, heads, HEAD_DIM)
        kv_s.append(jnp.stack([ks, vs], axis=2))
    conv_p = u[:n_p].reshape(batch, seq, c)[:, seq - (CONV_K - 1):]
    conv_s = jnp.stack([state_conv[:, 1], u_s], axis=1)
    return y_p, y_s, kv_p, conv_p, kv_s, conv_s


def kernel(x_prompt, x_sample, cache_kv_w128, cache_kv_w512, cache_kv_w2048, state_conv, norm1_gain,
           w_in, q_norm_gain, k_norm_gain, conv_w, w_attn_out, w_conv_out, w_out, norm2_gain,
           w_peer_q, peer_subkeys, peer_u, peer_v):
    depth = w_in.shape[0]
    yp, ys = x_prompt, x_sample
    kv_p = [[] for _ in range(N_GROUPS)]
    kv_s = [[] for _ in range(N_GROUPS)]
    conv_p, conv_s = [], []
    for layer in range(depth):
        caches = (cache_kv_w128, cache_kv_w512, cache_kv_w2048)
        yp, ys, kvp, cp, kvs, cs = _layer(
            yp, ys, caches, layer, state_conv[layer], norm1_gain[layer], w_in[layer], q_norm_gain[layer],
            k_norm_gain[layer], conv_w[layer], w_attn_out[layer], w_conv_out[layer], w_out[layer],
            norm2_gain[layer], w_peer_q[layer], peer_subkeys[layer], peer_u[layer], peer_v[layer])
        for g in range(N_GROUPS):
            kv_p[g].append(kvp[g])
            kv_s[g].append(kvs[g])
        conv_p.append(cp)
        conv_s.append(cs)
    return (yp, ys,
            jnp.stack(kv_p[0]), jnp.stack(kv_p[1]), jnp.stack(kv_p[2]), jnp.stack(conv_p),
            jnp.stack(kv_s[0]), jnp.stack(kv_s[1]), jnp.stack(kv_s[2]), jnp.stack(conv_s))
```
